```python
import math
import jax, jax.numpy as jnp
from jax import lax
import numpy as np

D_MODEL = 1024
BATCH = 8
SEQ = 4096
DEPTH = 2

MIX_WIDTH = D_MODEL
MLA_HEADS = 8
MLA_NOPE_DIM = 64
MLA_ROPE_DIM = 32
MLA_V_DIM = 64
MLA_Q_LORA = 384
MLA_KV_LORA = 256
MLA_WIDTH = MLA_HEADS * MLA_V_DIM
ROPE_THETA = 10000.0
Q_BLOCK = 128
CONV_CH = MIX_WIDTH - MLA_WIDTH
CONV_GROUPS = 8
CONV_K = 31
IN_COLS = MLA_Q_LORA + MLA_KV_LORA + MLA_ROPE_DIM + 2 * CONV_CH
D_FF = 2816
FFN_CONV_K = 3
EPS = 1e-6

kernel_name = "hybrid_mla_conformer_convffn_encoder"


def _rmsnorm(x, g):
    xf = x.astype(jnp.float32)
    y = xf * lax.rsqrt(jnp.mean(xf * xf, axis=-1, keepdims=True) + EPS)
    return (y * g.astype(jnp.float32)).astype(x.dtype)


def _layernorm(x, g, b):
    xf = x.astype(jnp.float32)
    mu = jnp.mean(xf, axis=-1, keepdims=True)
    var = jnp.mean(jnp.square(xf - mu), axis=-1, keepdims=True)
    y = (xf - mu) * lax.rsqrt(var + EPS)
    return (y * g.astype(jnp.float32) + b.astype(jnp.float32)).astype(x.dtype)


def _dwconv(x, w, b):
    k = w.shape[0]
    pad = k // 2
    y = lax.conv_general_dilated(
        x, w[:, None, :].astype(x.dtype), window_strides=(1,),
        padding=[(pad, pad)], dimension_numbers=("NWC", "WIO", "NWC"),
        feature_group_count=x.shape[-1])
    return y + b.astype(x.dtype)


def _rope_tables(positions):
    inv_freq = 1.0 / (ROPE_THETA ** (jnp.arange(0, MLA_ROPE_DIM, 2, dtype=jnp.float32) / MLA_ROPE_DIM))
    ang = positions.astype(jnp.float32)[..., None] * inv_freq
    return jnp.cos(ang), jnp.sin(ang)


def _apply_rope(t, cos, sin):
    half = t.shape[-1] // 2
    t1, t2 = t[..., :half], t[..., half:]
    cos = cos.astype(t.dtype)
    sin = sin.astype(t.dtype)
    return jnp.concatenate([t1 * cos - t2 * sin, t2 * cos + t1 * sin], axis=-1)


def _mla_attention(q_nope, q_rope, k_nope, k_rope, v):
    b, s, h, dn = q_nope.shape
    nb = s // Q_BLOCK
    scale = 1.0 / math.sqrt(dn + q_rope.shape[-1])

    def to_blocks(t):
        return jnp.moveaxis(t.reshape(b, nb, Q_BLOCK, *t.shape[2:]), 1, 0)

    def block(args):
        qn, qr = args
        sc = (jnp.einsum("bqhd,bkhd->bhqk", qn, k_nope)
              + jnp.einsum("bqhr,bkr->bhqk", qr, k_rope))
        p = jax.nn.softmax(sc.astype(jnp.float32) * scale, axis=-1).astype(v.dtype)
        return jnp.einsum("bhqk,bkhd->bqhd", p, v)

    o = lax.map(block, (to_blocks(q_nope), to_blocks(q_rope)))
    return jnp.moveaxis(o, 0, 1).reshape(b, s, h * v.shape[-1])


def setup_inputs(seed: int = 0) -> dict:
    key = jax.random.key(seed)
    ks = jax.random.split(key, 24)
    f32 = jnp.float32

    def w(k, shape, fan_in, gain=1.0):
        return jax.random.normal(k, shape, f32) * (gain * fan_in ** -0.5)

    def gain(k, shape):
        return 1.0 + 0.02 * jax.random.normal(k, shape, f32)

    def bias(k, shape):
        return 0.02 * jax.random.normal(k, shape, f32)

    x = jax.random.normal(ks[0], (BATCH, SEQ, D_MODEL), f32)
    offs = jax.random.randint(ks[1], (BATCH, 1), 0, 4096, dtype=jnp.int32)
    positions = (jnp.arange(SEQ, dtype=jnp.int32)[None, :] + offs).astype(jnp.int32)
    return {
        "x": x,
        "positions": positions,
        "g_mix": gain(ks[2], (DEPTH, D_MODEL)),
        "w_in": w(ks[3], (DEPTH, D_MODEL, IN_COLS), D_MODEL),
        "g_q": gain(ks[4], (DEPTH, MLA_Q_LORA)),
        "w_uq": w(ks[5], (DEPTH, MLA_Q_LORA, MLA_HEADS * (MLA_NOPE_DIM + MLA_ROPE_DIM)), MLA_Q_LORA),
        "g_kv": gain(ks[6], (DEPTH, MLA_KV_LORA)),
        "w_ukv": w(ks[7], (DEPTH, MLA_KV_LORA, MLA_HEADS * (MLA_NOPE_DIM + MLA_V_DIM)), MLA_KV_LORA),
        "w_dw_conv": w(ks[8], (DEPTH, CONV_K, CONV_CH), CONV_K),
        "b_dw_conv": bias(ks[9], (DEPTH, CONV_CH)),
        "g_conv_ln": gain(ks[10], (DEPTH, CONV_CH)),
        "b_conv_ln": bias(ks[11], (DEPTH, CONV_CH)),
        "w_o": w(ks[12], (DEPTH, MIX_WIDTH, D_MODEL), MIX_WIDTH, 0.5),
        "g_ffn": gain(ks[13], (DEPTH, D_MODEL)),
        "w_up": w(ks[14], (DEPTH, D_MODEL, 2 * D_FF), D_MODEL),
        "w_dw_ffn": w(ks[15], (DEPTH, FFN_CONV_K, 2 * D_FF), FFN_CONV_K),
        "b_dw_ffn": bias(ks[16], (DEPTH, 2 * D_FF)),
        "w_down": w(ks[17], (DEPTH, D_FF, D_MODEL), D_FF, 0.5),
        "g_final": gain(ks[18], (D_MODEL,)),
    }


def reference(x, positions, g_mix, w_in, g_q, w_uq, g_kv, w_ukv, w_dw_conv, b_dw_conv,
              g_conv_ln, b_conv_ln, w_o, g_ffn, w_up, w_dw_ffn, b_dw_ffn, w_down, g_final):
    b, s, _ = x.shape
    cos, sin = _rope_tables(positions)
    cos_h, sin_h = cos[:, :, None, :], sin[:, :, None, :]
    o1 = MLA_Q_LORA
    o2 = o1 + MLA_KV_LORA
    o3 = o2 + MLA_ROPE_DIM
    for l in range(DEPTH):
        h = _rmsnorm(x, g_mix[l])
        p = h @ w_in[l].astype(h.dtype)
        c_q = _rmsnorm(p[..., :o1], g_q[l])
        q = (c_q @ w_uq[l].astype(h.dtype)).reshape(b, s, MLA_HEADS, MLA_NOPE_DIM + MLA_ROPE_DIM)
        q_nope = q[..., :MLA_NOPE_DIM]
        q_rope = _apply_rope(q[..., MLA_NOPE_DIM:], cos_h, sin_h)
        c_kv = _rmsnorm(p[..., o1:o2], g_kv[l])
        kv = (c_kv @ w_ukv[l].astype(h.dtype)).reshape(b, s, MLA_HEADS, MLA_NOPE_DIM + MLA_V_DIM)
        k_nope = kv[..., :MLA_NOPE_DIM]
        v = kv[..., MLA_NOPE_DIM:]
        k_rope = _apply_rope(p[..., o2:o3], cos, sin)
        y_attn = _mla_attention(q_nope, q_rope, k_nope, k_rope, v)
        a, gt = jnp.split(p[..., o3:], 2, axis=-1)
        u = a * jax.nn.sigmoid(gt)
        u = _dwconv(u, w_dw_conv[l], b_dw_conv[l])
        u = jax.nn.silu(_layernorm(u, g_conv_ln[l], b_conv_ln[l]))
        y = jnp.concatenate([y_attn, u], axis=-1) @ w_o[l].astype(h.dtype)
        x = x + y
        h2 = _rmsnorm(x, g_ffn[l])
        z = _dwconv(h2 @ w_up[l].astype(h2.dtype), w_dw_ffn[l], b_dw_ffn[l])
        zg, zv = jnp.split(z, 2, axis=-1)
        x = x + (jax.nn.silu(zg) * zv) @ w_down[l].astype(h2.dtype)
    return _rmsnorm(x, g_final)
```

```python
import functools
import math

import jax
import jax.numpy as jnp
from jax import lax
from jax.experimental import pallas as pl
from jax.experimental.pallas import tpu as pltpu

D_MODEL = 1024
N_HEADS = 8
D_NOPE = 64
D_ROPE = 32
D_V = 64
Q_LORA = 384
KV_LORA = 256
CONV_CH = 512
CONV_K = 31
D_FF = 2816
FFN_K = 3
ROPE_THETA = 10000.0
EPS = 1e-6

HEAD_PAD = 128
IN_EXT = Q_LORA + KV_LORA + 2 * CONV_CH + HEAD_PAD
CONV_HALO = 16
FFN_HALO = 8
FF_CHUNK = 256

TM_PROJ = 512
TQ = 256
TK = 256
TM_MIX = 256
TM_FFN = 512

VMEM_LIMIT = 56 * 1024 * 1024

F32 = jnp.float32
BF16 = jnp.bfloat16


def _cparams(n_axes):
    return pltpu.CompilerParams(
        dimension_semantics=("arbitrary",) * n_axes, vmem_limit_bytes=VMEM_LIMIT)


def _full(shape):
    return pl.BlockSpec(shape, lambda *_: (0,) * len(shape))


def _rms(v, g):
    return v * lax.rsqrt(jnp.mean(v * v, axis=-1, keepdims=True) + EPS) * g


def _table_kernel(pos_ref, invf_ref, cos_ref, sin_ref, tabk_ref):
    n_b = pos_ref.shape[0]
    s = pos_ref.shape[2]
    zeros = jnp.zeros((D_NOPE, s), F32)
    for b in range(n_b):
        ang = pos_ref[b].astype(F32) * invf_ref[...]
        c = jnp.cos(ang)
        sn = jnp.sin(ang)
        cos_ref[b] = c
        sin_ref[b] = sn
        tabk_ref[b] = jnp.concatenate([zeros, c, c, sn, sn], axis=0).T


def _rope_tables(positions):
    b, s = positions.shape
    half = D_ROPE // 2
    inv_freq = 1.0 / (ROPE_THETA ** (jnp.arange(0, D_ROPE, 2, dtype=F32) / D_ROPE))
    invf = jnp.broadcast_to(inv_freq[:, None], (half, s))
    return pl.pallas_call(
        _table_kernel,
        out_shape=(jax.ShapeDtypeStruct((b, half, s), F32),
                   jax.ShapeDtypeStruct((b, half, s), F32),
                   jax.ShapeDtypeStruct((b, s, HEAD_PAD), F32)),
        grid=(1,),
        in_specs=[_full((b, 1, s)), _full((half, s))],
        out_specs=(_full((b, half, s)), _full((b, half, s)), _full((b, s, HEAD_PAD))),
        compiler_params=_cparams(1),
        name="rope_tables",
    )(positions.reshape(b, 1, s), invf)


def _proj_kernel(x_ref, cos_ref, sin_ref, tabk_ref, gmix_ref, win_ref, gq_ref, wuqt_ref,
                 gkv_ref, wk_ref, wvt_ref, qt_ref, k_ref, vt_ref, u_ref):
    tm = x_ref.shape[1]
    scale = 1.0 / math.sqrt(D_NOPE + D_ROPE)
    h = _rms(x_ref[0], gmix_ref[...]).astype(BF16)
    p = jnp.dot(h, win_ref[...], preferred_element_type=F32)
    o1 = Q_LORA
    o2 = o1 + KV_LORA
    o3 = o2 + CONV_CH
    o4 = o3 + CONV_CH
    c_q = _rms(p[:, :o1], gq_ref[...]).astype(BF16)
    c_kv = _rms(p[:, o1:o2], gkv_ref[...]).astype(BF16)

    a = p[:, o2:o3]
    gt = p[:, o3:o4]
    u_ref[0] = a * jax.nn.sigmoid(gt)

    nt = (((1,), (1,)), ((), ()))
    q2 = lax.dot_general(wuqt_ref[...], c_q, nt, preferred_element_type=F32)
    c = cos_ref[0] * scale
    s = sin_ref[0] * scale
    cos_t = jnp.concatenate([jnp.full((D_NOPE, tm), scale, F32), c, c,
                             jnp.zeros((HEAD_PAD - D_NOPE - D_ROPE, tm), F32)], axis=0)
    sin_t = jnp.concatenate([jnp.zeros((D_NOPE, tm), F32), s, s,
                             jnp.zeros((HEAD_PAD - D_NOPE - D_ROPE, tm), F32)], axis=0)
    n_q = N_HEADS * HEAD_PAD
    for hh in range(N_HEADS):
        lo = hh * HEAD_PAD
        qh = q2[lo:lo + HEAD_PAD] * cos_t + q2[n_q + lo:n_q + lo + HEAD_PAD] * sin_t
        qt_ref[0, hh] = qh.astype(BF16)

    kn = jnp.dot(c_kv, wk_ref[...], preferred_element_type=F32)
    kr = p[:, o4:o4 + HEAD_PAD] * tabk_ref[0]
    kr = kr + pltpu.roll(kr, HEAD_PAD - D_ROPE, axis=1)
    lane = lax.broadcasted_iota(jnp.int32, kr.shape, 1)
    kr = jnp.where((lane >= D_NOPE) & (lane < D_NOPE + D_ROPE), kr, 0.0)
    for hh in range(N_HEADS):
        lo = hh * HEAD_PAD
        k_ref[0, hh] = (kn[:, lo:lo + HEAD_PAD] + kr).astype(BF16)

    vt = lax.dot_general(wvt_ref[...], c_kv, nt, preferred_element_type=F32)
    for hh in range(N_HEADS):
        vt_ref[0, hh] = vt[hh * D_V:(hh + 1) * D_V].astype(BF16)


def _proj(x, cos_t, sin_t, tab_k, g_mix, w_in_ext, g_q, w_uqt, g_kv, w_k, w_vt):
    b, s, d = x.shape
    tm = TM_PROJ
    half = D_ROPE // 2
    tok = lambda bi, i: (bi, i, 0)
    tok_t = lambda bi, i: (bi, 0, i)
    return pl.pallas_call(
        _proj_kernel,
        out_shape=(jax.ShapeDtypeStruct((b, N_HEADS, HEAD_PAD, s), BF16),
                   jax.ShapeDtypeStruct((b, N_HEADS, s, HEAD_PAD), BF16),
                   jax.ShapeDtypeStruct((b, N_HEADS, D_V, s), BF16),
                   jax.ShapeDtypeStruct((b, s, CONV_CH), F32)),
        grid=(b, s // tm),
        in_specs=[pl.BlockSpec((1, tm, d), tok),
                  pl.BlockSpec((1, half, tm), tok_t),
                  pl.BlockSpec((1, half, tm), tok_t),
                  pl.BlockSpec((1, tm, HEAD_PAD), tok),
                  _full(g_mix.shape), _full(w_in_ext.shape), _full(g_q.shape),
                  _full(w_uqt.shape), _full(g_kv.shape), _full(w_k.shape), _full(w_vt.shape)],
        out_specs=(pl.BlockSpec((1, N_HEADS, HEAD_PAD, tm), lambda bi, i: (bi, 0, 0, i)),
                   pl.BlockSpec((1, N_HEADS, tm, HEAD_PAD), lambda bi, i: (bi, 0, i, 0)),
                   pl.BlockSpec((1, N_HEADS, D_V, tm), lambda bi, i: (bi, 0, 0, i)),
                   pl.BlockSpec((1, tm, CONV_CH), tok)),
        compiler_params=_cparams(2),
        name="proj",
    )(x, cos_t, sin_t, tab_k, g_mix, w_in_ext, g_q, w_uqt, g_kv, w_k, w_vt)


def _attn_kernel(qt_ref, k_ref, vt_ref, yt_ref):
    tq = qt_ref.shape[3]
    s_len = k_ref.shape[2]
    n_chunks = s_len // TK

    def head_body(hh, _):
        qh = qt_ref[0, hh]

        def kv_body(ci, carry):
            m, l, acc = carry
            off = pl.multiple_of(ci * TK, TK)
            kc = k_ref[0, hh, pl.ds(off, TK), :]
            st = jnp.dot(kc, qh, preferred_element_type=F32)
            m_new = jnp.maximum(m, jnp.max(st, axis=0, keepdims=True))
            alpha = jnp.exp(m - m_new)
            pt = jnp.exp(st - m_new)
            l = alpha * l + jnp.sum(pt, axis=0, keepdims=True)
            vc = vt_ref[0, hh, :, pl.ds(off, TK)]
            acc = alpha * acc + jnp.dot(vc, pt.astype(BF16), preferred_element_type=F32)
            return m_new, l, acc

        init = (jnp.full((1, tq), -jnp.inf, F32), jnp.zeros((1, tq), F32),
                jnp.zeros((D_V, tq), F32))
        _, l, acc = lax.fori_loop(0, n_chunks, kv_body, init)
        yt_ref[0, hh] = (acc / l).astype(BF16)
        return 0

    lax.fori_loop(0, N_HEADS, head_body, 0)


def _attention(qt, k, vt):
    b, nh, _, s = qt.shape
    return pl.pallas_call(
        _attn_kernel,
        out_shape=jax.ShapeDtypeStruct((b, nh, D_V, s), BF16),
        grid=(b, s // TQ),
        in_specs=[pl.BlockSpec((1, nh, HEAD_PAD, TQ), lambda bi, i: (bi, 0, 0, i)),
                  pl.BlockSpec((1, nh, s, HEAD_PAD), lambda bi, i: (bi, 0, 0, 0)),
                  pl.BlockSpec((1, nh, D_V, s), lambda bi, i: (bi, 0, 0, 0))],
        out_specs=pl.BlockSpec((1, nh, D_V, TQ), lambda bi, i: (bi, 0, 0, i)),
        compiler_params=_cparams(2),
        name="attn",
    )(qt, k, vt)


def _mix_kernel(yt_ref, u_ref, up_ref, un_ref, x_ref, wdw_ref, bdw_ref, gln_ref, bln_ref,
                woa_ref, woc_ref, o_ref, ext_ref):
    tm = u_ref.shape[1]
    i = pl.program_id(1)
    last = pl.num_programs(1) - 1
    ext_ref[0:CONV_HALO] = jnp.where(i > 0, up_ref[0], 0.0)
    ext_ref[CONV_HALO:CONV_HALO + tm] = u_ref[0]
    ext_ref[CONV_HALO + tm:CONV_HALO + tm + CONV_HALO] = jnp.where(i < last, un_ref[0], 0.0)

    base = CONV_HALO - CONV_K // 2
    acc = jnp.broadcast_to(bdw_ref[...], (tm, CONV_CH))
    for kk in range(CONV_K):
        acc = acc + wdw_ref[kk:kk + 1, :] * ext_ref[base + kk:base + kk + tm, :]

    mu = jnp.mean(acc, axis=-1, keepdims=True)
    cen = acc - mu
    var = jnp.mean(cen * cen, axis=-1, keepdims=True)
    y = cen * lax.rsqrt(var + EPS) * gln_ref[...] + bln_ref[...]
    conv = (y * jax.nn.sigmoid(y)).astype(BF16)

    yt = yt_ref[0].reshape(N_HEADS * D_V, tm)
    tn = (((0,), (0,)), ((), ()))
    out = lax.dot_general(yt, woa_ref[...], tn, preferred_element_type=F32)
    out = out + jnp.dot(conv, woc_ref[...], preferred_element_type=F32)
    o_ref[0] = x_ref[0] + out


def _mix(yt, u, x, w_dw, b_dw, g_ln, b_ln, w_oa, w_oc):
    b, s, d = x.shape
    tm = TM_MIX
    r = tm // CONV_HALO
    n_halo = s // CONV_HALO
    tok = lambda bi, i: (bi, i, 0)
    prev = lambda bi, i: (bi, jnp.maximum(i * r - 1, 0), 0)
    nxt = lambda bi, i: (bi, jnp.minimum((i + 1) * r, n_halo - 1), 0)
    return pl.pallas_call(
        _mix_kernel,
        out_shape=jax.ShapeDtypeStruct((b, s, d), F32),
        grid=(b, s // tm),
        in_specs=[pl.BlockSpec((1, N_HEADS, D_V, tm), lambda bi, i: (bi, 0, 0, i)),
                  pl.BlockSpec((1, tm, CONV_CH), tok),
                  pl.BlockSpec((1, CONV_HALO, CONV_CH), prev),
                  pl.BlockSpec((1, CONV_HALO, CONV_CH), nxt),
                  pl.BlockSpec((1, tm, d), tok),
                  _full(w_dw.shape), _full(b_dw.shape), _full(g_ln.shape), _full(b_ln.shape),
                  _full(w_oa.shape), _full(w_oc.shape)],
        out_specs=pl.BlockSpec((1, tm, d), tok),
        scratch_shapes=[pltpu.VMEM((tm + 2 * CONV_HALO, CONV_CH), F32)],
        compiler_params=_cparams(2),
        name="mix",
    )(yt, u, u, u, x, w_dw, b_dw, g_ln, b_ln, w_oa, w_oc)


def _ffn_kernel(x_ref, xp_ref, xn_ref, g_ref, wup_ref, wdw_ref, bdw_ref, wdn_ref, gfin_ref,
                o_ref, h_ref, zg_ref, zv_ref, *, final_norm):
    tm = x_ref.shape[1]
    i = pl.program_id(1)
    last = pl.num_programs(1) - 1
    g = g_ref[...]
    hp = jnp.where(i > 0, _rms(xp_ref[0], g), 0.0)
    hn = jnp.where(i < last, _rms(xn_ref[0], g), 0.0)
    h_ref[0:FFN_HALO] = hp.astype(BF16)
    h_ref[FFN_HALO:FFN_HALO + tm] = _rms(x_ref[0], g).astype(BF16)
    h_ref[FFN_HALO + tm:FFN_HALO + tm + FFN_HALO] = hn.astype(BF16)
    h = h_ref[...]

    base = FFN_HALO - FFN_K // 2

    def conv3(z_ref, col):
        out = bdw_ref[:, col:col + FF_CHUNK]
        for kk in range(FFN_K):
            out = out + wdw_ref[kk:kk + 1, col:col + FF_CHUNK] * z_ref[base + kk:base + kk + tm, :]
        return out

    acc = jnp.zeros((tm, D_MODEL), F32)
    for c in range(D_FF // FF_CHUNK):
        cg = c * FF_CHUNK
        cv = D_FF + c * FF_CHUNK
        zg_ref[...] = jnp.dot(h, wup_ref[:, cg:cg + FF_CHUNK], preferred_element_type=F32)
        zv_ref[...] = jnp.dot(h, wup_ref[:, cv:cv + FF_CHUNK], preferred_element_type=F32)
        zg = conv3(zg_ref, cg)
        zv = conv3(zv_ref, cv)
        act = (zg * jax.nn.sigmoid(zg) * zv).astype(BF16)
        acc = acc + jnp.dot(act, wdn_ref[cg:cg + FF_CHUNK, :], preferred_element_type=F32)

    out = x_ref[0] + acc
    if final_norm:
        out = _rms(out, gfin_ref[...])
    o_ref[0] = out


def _ffn(x, g_ffn, w_up, w_dw, b_dw, w_down, g_final, final_norm):
    b, s, d = x.shape
    tm = TM_FFN
    r = tm // FFN_HALO
    n_halo = s // FFN_HALO
    tok = lambda bi, i: (bi, i, 0)
    prev = lambda bi, i: (bi, jnp.maximum(i * r - 1, 0), 0)
    nxt = lambda bi, i: (bi, jnp.minimum((i + 1) * r, n_halo - 1), 0)
    single = dict(pipeline_mode=pl.Buffered(1))
    return pl.pallas_call(
        functools.partial(_ffn_kernel, final_norm=final_norm),
        out_shape=jax.ShapeDtypeStruct((b, s, d), F32),
        grid=(b, s // tm),
        in_specs=[pl.BlockSpec((1, tm, d), tok),
                  pl.BlockSpec((1, FFN_HALO, d), prev),
                  pl.BlockSpec((1, FFN_HALO, d), nxt),
                  _full(g_ffn.shape),
                  pl.BlockSpec(w_up.shape, lambda *_: (0, 0), **single),
                  _full(w_dw.shape), _full(b_dw.shape),
                  pl.BlockSpec(w_down.shape, lambda *_: (0, 0), **single),
                  _full(g_final.shape)],
        out_specs=pl.BlockSpec((1, tm, d), tok),
        scratch_shapes=[pltpu.VMEM((tm + 2 * FFN_HALO, d), BF16),
                        pltpu.VMEM((tm + 2 * FFN_HALO, FF_CHUNK), F32),
                        pltpu.VMEM((tm + 2 * FFN_HALO, FF_CHUNK), F32)],
        compiler_params=_cparams(2),
        name="ffn_final" if final_norm else "ffn",
    )(x, x, x, g_ffn, w_up, w_dw, b_dw, w_down, g_final)


def _rot_cols(w):
    half = w.shape[-1] // 2
    return jnp.concatenate([-w[..., half:], w[..., :half]], axis=-1)


def _prep_layer(w_in, w_uq, w_ukv, w_o):
    o1 = Q_LORA
    o2 = o1 + KV_LORA
    o3 = o2 + D_ROPE
    w_kr = w_in[:, o2:o3]
    kr_block = jnp.concatenate(
        [jnp.zeros((D_MODEL, D_NOPE), F32), w_kr, _rot_cols(w_kr)], axis=1)
    w_in_ext = jnp.concatenate([w_in[:, :o2], w_in[:, o3:], kr_block], axis=1).astype(BF16)

    wq = w_uq.reshape(Q_LORA, N_HEADS, D_NOPE + D_ROPE)
    pad = jnp.zeros((Q_LORA, N_HEADS, HEAD_PAD - D_NOPE - D_ROPE), F32)
    q_main = jnp.concatenate([wq, pad], axis=2).reshape(Q_LORA, N_HEADS * HEAD_PAD)
    q_rot = jnp.concatenate(
        [jnp.zeros((Q_LORA, N_HEADS, D_NOPE), F32), _rot_cols(wq[:, :, D_NOPE:]), pad],
        axis=2).reshape(Q_LORA, N_HEADS * HEAD_PAD)
    w_uqt = jnp.concatenate([q_main, q_rot], axis=1).T.astype(BF16)

    wkv = w_ukv.reshape(KV_LORA, N_HEADS, D_NOPE + D_V)
    w_k = jnp.concatenate(
        [wkv[:, :, :D_NOPE], jnp.zeros((KV_LORA, N_HEADS, HEAD_PAD - D_NOPE), F32)],
        axis=2).reshape(KV_LORA, N_HEADS * HEAD_PAD).astype(BF16)
    w_vt = wkv[:, :, D_NOPE:].reshape(KV_LORA, N_HEADS * D_V).T.astype(BF16)

    w_oa = w_o[:N_HEADS * D_V].astype(BF16)
    w_oc = w_o[N_HEADS * D_V:].astype(BF16)
    return w_in_ext, w_uqt, w_k, w_vt, w_oa, w_oc


def kernel(x, positions, g_mix, w_in, g_q, w_uq, g_kv, w_ukv, w_dw_conv, b_dw_conv, g_conv_ln, b_conv_ln, w_o, g_ffn, w_up, w_dw_ffn, b_dw_ffn, w_down, g_final):
    depth = w_in.shape[0]
    cos_t, sin_t, tab_k = _rope_tables(positions)
    row = lambda v: v.reshape(1, -1)
    for l in range(depth):
        w_in_ext, w_uqt, w_k, w_vt, w_oa, w_oc = _prep_layer(w_in[l], w_uq[l], w_ukv[l], w_o[l])
        qt, k, vt, u = _proj(x, cos_t, sin_t, tab_k, row(g_mix[l]), w_in_ext, row(g_q[l]),
                             w_uqt, row(g_kv[l]), w_k, w_vt)
        yt = _attention(qt, k, vt)
        x = _mix(yt, u, x, w_dw_conv[l], row(b_dw_conv[l]), row(g_conv_ln[l]),
                 row(b_conv_ln[l]), w_oa, w_oc)
        x = _ffn(x, row(g_ffn[l]), w_up[l].astype(BF16), w_dw_ffn[l], row(b_dw_ffn[l]),
                 w_down[l].astype(BF16), row(g_final), final_norm=(l == depth - 1))
    return x
```

```python
import functools
import math

import jax
import jax.numpy as jnp
from jax import lax
from jax.experimental import pallas as pl
from jax.experimental.pallas import tpu as pltpu

D_MODEL = 1024
N_HEADS = 8
D_NOPE = 64
D_ROPE = 32
D_V = 64
Q_LORA = 384
KV_LORA = 256
CONV_CH = 512
CONV_K = 31
D_FF = 2816
FFN_K = 3
ROPE_THETA = 10000.0
EPS = 1e-6

HEAD_PAD = 128
IN_EXT = Q_LORA + KV_LORA + 2 * CONV_CH + HEAD_PAD
CONV_HALO = 16
FFN_HALO = 8
FF_CHUNK = 256

TM_PROJ = 512
TQ = 256
TK = 256
TM_MIX = 256
TM_FFN = 512

VMEM_LIMIT = 56 * 1024 * 1024

F32 = jnp.float32
BF16 = jnp.bfloat16


def _cparams(n_axes):
    return pltpu.CompilerParams(
        dimension_semantics=("arbitrary",) * n_axes, vmem_limit_bytes=VMEM_LIMIT)


def _full(shape):
    return pl.BlockSpec(shape, lambda *_: (0,) * len(shape))


def _rms(v, g):
    return v * lax.rsqrt(jnp.mean(v * v, axis=-1, keepdims=True) + EPS) * g


def _table_kernel(pos_ref, invf_ref, cos_ref, sin_ref, tabk_ref):
    n_b = pos_ref.shape[0]
    s = pos_ref.shape[2]
    zeros = jnp.zeros((D_NOPE, s), F32)
    for b in range(n_b):
        ang = pos_ref[b].astype(F32) * invf_ref[...]
        c = jnp.cos(ang)
        sn = jnp.sin(ang)
        cos_ref[b] = c
        sin_ref[b] = sn
        tabk_ref[b] = jnp.concatenate([zeros, c, c, sn, sn], axis=0).T


def _rope_tables(positions):
    b, s = positions.shape
    half = D_ROPE // 2
    inv_freq = 1.0 / (ROPE_THETA ** (jnp.arange(0, D_ROPE, 2, dtype=F32) / D_ROPE))
    invf = jnp.broadcast_to(inv_freq[:, None], (half, s))
    return pl.pallas_call(
        _table_kernel,
        out_shape=(jax.ShapeDtypeStruct((b, half, s), F32),
                   jax.ShapeDtypeStruct((b, half, s), F32),
                   jax.ShapeDtypeStruct((b, s, HEAD_PAD), F32)),
        grid=(1,),
        in_specs=[_full((b, 1, s)), _full((half, s))],
        out_specs=(_full((b, half, s)), _full((b, half, s)), _full((b, s, HEAD_PAD))),
        compiler_params=_cparams(1),
        name="rope_tables",
    )(positions.reshape(b, 1, s), invf)


def _proj_kernel(x_ref, cos_ref, sin_ref, tabk_ref, gmix_ref, win_ref, gq_ref, wuqt_ref,
                 gkv_ref, wk_ref, wvt_ref, qt_ref, k_ref, vt_ref, u_ref):
    tm = x_ref.shape[1]
    scale = 1.0 / math.sqrt(D_NOPE + D_ROPE)
    h = _rms(x_ref[0], gmix_ref[...]).astype(BF16)
    p = jnp.dot(h, win_ref[...], preferred_element_type=F32)
    o1 = Q_LORA
    o2 = o1 + KV_LORA
    o3 = o2 + CONV_CH
    o4 = o3 + CONV_CH
    c_q = _rms(p[:, :o1], gq_ref[...]).astype(BF16)
    c_kv = _rms(p[:, o1:o2], gkv_ref[...]).astype(BF16)

    a = p[:, o2:o3]
    gt = p[:, o3:o4]
    u_ref[0] = a * jax.nn.sigmoid(gt)

    nt = (((1,), (1,)), ((), ()))
    q2 = lax.dot_general(wuqt_ref[...], c_q, nt, preferred_element_type=F32)
    c = cos_ref[0] * scale
    s = sin_ref[0] * scale
    cos_t = jnp.concatenate([jnp.full((D_NOPE, tm), scale, F32), c, c,
                             jnp.zeros((HEAD_PAD - D_NOPE - D_ROPE, tm), F32)], axis=0)
    sin_t = jnp.concatenate([jnp.zeros((D_NOPE, tm), F32), s, s,
                             jnp.zeros((HEAD_PAD - D_NOPE - D_ROPE, tm), F32)], axis=0)
    n_q = N_HEADS * HEAD_PAD
    for hh in range(N_HEADS):
        lo = hh * HEAD_PAD
        qh = q2[lo:lo + HEAD_PAD] * cos_t + q2[n_q + lo:n_q + lo + HEAD_PAD] * sin_t
        qt_ref[0, hh] = qh.astype(BF16)

    kn = jnp.dot(c_kv, wk_ref[...], preferred_element_type=F32)
    kr = p[:, o4:o4 + HEAD_PAD] * tabk_ref[0]
    kr = kr + pltpu.roll(kr, HEAD_PAD - D_ROPE, axis=1)
    lane = lax.broadcasted_iota(jnp.int32, kr.shape, 1)
    kr = jnp.where((lane >= D_NOPE) & (lane < D_NOPE + D_ROPE), kr, 0.0)
    for hh in range(N_HEADS):
        lo = hh * HEAD_PAD
        k_ref[0, hh] = (kn[:, lo:lo + HEAD_PAD] + kr).astype(BF16)

    vt = lax.dot_general(wvt_ref[...], c_kv, nt, preferred_element_type=F32)
    for hh in range(N_HEADS):
        vt_ref[0, hh] = vt[hh * D_V:(hh + 1) * D_V].astype(BF16)


def _proj(x, cos_t, sin_t, tab_k, g_mix, w_in_ext, g_q, w_uqt, g_kv, w_k, w_vt):
    b, s, d = x.shape
    tm = TM_PROJ
    half = D_ROPE // 2
    tok = lambda bi, i: (bi, i, 0)
    tok_t = lambda bi, i: (bi, 0, i)
    return pl.pallas_call(
        _proj_kernel,
        out_shape=(jax.ShapeDtypeStruct((b, N_HEADS, HEAD_PAD, s), BF16),
                   jax.ShapeDtypeStruct((b, N_HEADS, s, HEAD_PAD), BF16),
                   jax.ShapeDtypeStruct((b, N_HEADS, D_V, s), BF16),
                   jax.ShapeDtypeStruct((b, s, CONV_CH), F32)),
        grid=(b, s // tm),
        in_specs=[pl.BlockSpec((1, tm, d), tok),
                  pl.BlockSpec((1, half, tm), tok_t),
                  pl.BlockSpec((1, half, tm), tok_t),
                  pl.BlockSpec((1, tm, HEAD_PAD), tok),
                  _full(g_mix.shape), _full(w_in_ext.shape), _full(g_q.shape),
                  _full(w_uqt.shape), _full(g_kv.shape), _full(w_k.shape), _full(w_vt.shape)],
        out_specs=(pl.BlockSpec((1, N_HEADS, HEAD_PAD, tm), lambda bi, i: (bi, 0, 0, i)),
                   pl.BlockSpec((1, N_HEADS, tm, HEAD_PAD), lambda bi, i: (bi, 0, i, 0)),
                   pl.BlockSpec((1, N_HEADS, D_V, tm), lambda bi, i: (bi, 0, 0, i)),
                   pl.BlockSpec((1, tm, CONV_CH), tok)),
        compiler_params=_cparams(2),
        name="proj",
    )(x, cos_t, sin_t, tab_k, g_mix, w_in_ext, g_q, w_uqt, g_kv, w_k, w_vt)


def _attn_kernel(qt_ref, k_ref, vt_ref, yt_ref, s_ref):
    tq = qt_ref.shape[3]
    s_len = k_ref.shape[2]
    n_chunks = s_len // TK
    groups = TK // 8

    def head_body(hh, _):
        qh = qt_ref[0, hh]

        m8 = None
        for c in range(n_chunks):
            kc = k_ref[0, hh, c * TK:(c + 1) * TK, :]
            st = jnp.dot(kc, qh, preferred_element_type=F32)
            s_ref[c * TK:(c + 1) * TK, :] = st
            mc = jnp.max(st.reshape(groups, 8, tq), axis=0)
            m8 = mc if m8 is None else jnp.maximum(m8, mc)
        m = jnp.max(m8, axis=0, keepdims=True)

        l8 = jnp.zeros((8, tq), F32)
        acc = jnp.zeros((D_V, tq), F32)
        for c in range(n_chunks):
            pt = jnp.exp(s_ref[c * TK:(c + 1) * TK, :] - m)
            l8 = l8 + jnp.sum(pt.reshape(groups, 8, tq), axis=0)
            vc = vt_ref[0, hh, :, c * TK:(c + 1) * TK]
            acc = acc + jnp.dot(vc, pt.astype(BF16), preferred_element_type=F32)
        l = jnp.sum(l8, axis=0, keepdims=True)
        yt_ref[0, hh] = (acc / l).astype(BF16)
        return 0

    lax.fori_loop(0, N_HEADS, head_body, 0)


def _attention(qt, k, vt):
    b, nh, _, s = qt.shape
    return pl.pallas_call(
        _attn_kernel,
        out_shape=jax.ShapeDtypeStruct((b, nh, D_V, s), BF16),
        grid=(b, s // TQ),
        in_specs=[pl.BlockSpec((1, nh, HEAD_PAD, TQ), lambda bi, i: (bi, 0, 0, i)),
                  pl.BlockSpec((1, nh, s, HEAD_PAD), lambda bi, i: (bi, 0, 0, 0)),
                  pl.BlockSpec((1, nh, D_V, s), lambda bi, i: (bi, 0, 0, 0))],
        out_specs=pl.BlockSpec((1, nh, D_V, TQ), lambda bi, i: (bi, 0, 0, i)),
        scratch_shapes=[pltpu.VMEM((s, TQ), F32)],
        compiler_params=_cparams(2),
        name="attn",
    )(qt, k, vt)


def _mix_kernel(yt_ref, u_ref, up_ref, un_ref, x_ref, wdw_ref, bdw_ref, gln_ref, bln_ref,
                woa_ref, woc_ref, o_ref, ext_ref):
    tm = u_ref.shape[1]
    i = pl.program_id(1)
    last = pl.num_programs(1) - 1
    ext_ref[0:CONV_HALO] = jnp.where(i > 0, up_ref[0], 0.0)
    ext_ref[CONV_HALO:CONV_HALO + tm] = u_ref[0]
    ext_ref[CONV_HALO + tm:CONV_HALO + tm + CONV_HALO] = jnp.where(i < last, un_ref[0], 0.0)

    base = CONV_HALO - CONV_K // 2
    acc = jnp.broadcast_to(bdw_ref[...], (tm, CONV_CH))
    for kk in range(CONV_K):
        acc = acc + wdw_ref[kk:kk + 1, :] * ext_ref[base + kk:base + kk + tm, :]

    mu = jnp.mean(acc, axis=-1, keepdims=True)
    cen = acc - mu
    var = jnp.mean(cen * cen, axis=-1, keepdims=True)
    y = cen * lax.rsqrt(var + EPS) * gln_ref[...] + bln_ref[...]
    conv = (y * jax.nn.sigmoid(y)).astype(BF16)

    yt = yt_ref[0].reshape(N_HEADS * D_V, tm)
    tn = (((0,), (0,)), ((), ()))
    out = lax.dot_general(yt, woa_ref[...], tn, preferred_element_type=F32)
    out = out + jnp.dot(conv, woc_ref[...], preferred_element_type=F32)
    o_ref[0] = x_ref[0] + out


def _mix(yt, u, x, w_dw, b_dw, g_ln, b_ln, w_oa, w_oc):
    b, s, d = x.shape
    tm = TM_MIX
    r = tm // CONV_HALO
    n_halo = s // CONV_HALO
    tok = lambda bi, i: (bi, i, 0)
    prev = lambda bi, i: (bi, jnp.maximum(i * r - 1, 0), 0)
    nxt = lambda bi, i: (bi, jnp.minimum((i + 1) * r, n_halo - 1), 0)
    return pl.pallas_call(
        _mix_kernel,
        out_shape=jax.ShapeDtypeStruct((b, s, d), F32),
        grid=(b, s // tm),
        in_specs=[pl.BlockSpec((1, N_HEADS, D_V, tm), lambda bi, i: (bi, 0, 0, i)),
                  pl.BlockSpec((1, tm, CONV_CH), tok),
                  pl.BlockSpec((1, CONV_HALO, CONV_CH), prev),
                  pl.BlockSpec((1, CONV_HALO, CONV_CH), nxt),
                  pl.BlockSpec((1, tm, d), tok),
                  _full(w_dw.shape), _full(b_dw.shape), _full(g_ln.shape), _full(b_ln.shape),
                  _full(w_oa.shape), _full(w_oc.shape)],
        out_specs=pl.BlockSpec((1, tm, d), tok),
        scratch_shapes=[pltpu.VMEM((tm + 2 * CONV_HALO, CONV_CH), F32)],
        compiler_params=_cparams(2),
        name="mix",
    )(yt, u, u, u, x, w_dw, b_dw, g_ln, b_ln, w_oa, w_oc)


def _ffn_kernel(x_ref, xp_ref, xn_ref, g_ref, wup_ref, wdw_ref, bdw_ref, wdn_ref, gfin_ref,
                o_ref, h_ref, zg_ref, zv_ref, *, final_norm):
    tm = x_ref.shape[1]
    i = pl.program_id(1)
    last = pl.num_programs(1) - 1
    g = g_ref[...]
    hp = jnp.where(i > 0, _rms(xp_ref[0], g), 0.0)
    hn = jnp.where(i < last, _rms(xn_ref[0], g), 0.0)
    h_ref[0:FFN_HALO] = hp.astype(BF16)
    h_ref[FFN_HALO:FFN_HALO + tm] = _rms(x_ref[0], g).astype(BF16)
    h_ref[FFN_HALO + tm:FFN_HALO + tm + FFN_HALO] = hn.astype(BF16)
    h = h_ref[...]

    base = FFN_HALO - FFN_K // 2

    def conv3(z_ref, col):
        out = bdw_ref[:, col:col + FF_CHUNK]
        for kk in range(FFN_K):
            out = out + wdw_ref[kk:kk + 1, col:col + FF_CHUNK] * z_ref[base + kk:base + kk + tm, :]
        return out

    acc = jnp.zeros((tm, D_MODEL), F32)
    for c in range(D_FF // FF_CHUNK):
        cg = c * FF_CHUNK
        cv = D_FF + c * FF_CHUNK
        zg_ref[...] = jnp.dot(h, wup_ref[:, cg:cg + FF_CHUNK], preferred_element_type=F32)
        zv_ref[...] = jnp.dot(h, wup_ref[:, cv:cv + FF_CHUNK], preferred_element_type=F32)
        zg = conv3(zg_ref, cg)
        zv = conv3(zv_ref, cv)
        act = (zg * jax.nn.sigmoid(zg) * zv).astype(BF16)
        acc = acc + jnp.dot(act, wdn_ref[cg:cg + FF_CHUNK, :], preferred_element_type=F32)

    out = x_ref[0] + acc
    if final_norm:
        out = _rms(out, gfin_ref[...])
    o_ref[0] = out


def _ffn(x, g_ffn, w_up, w_dw, b_dw, w_down, g_final, final_norm):
    b, s, d = x.shape
    tm = TM_FFN
    r = tm // FFN_HALO
    n_halo = s // FFN_HALO
    tok = lambda bi, i: (bi, i, 0)
    prev = lambda bi, i: (bi, jnp.maximum(i * r - 1, 0), 0)
    nxt = lambda bi, i: (bi, jnp.minimum((i + 1) * r, n_halo - 1), 0)
    single = dict(pipeline_mode=pl.Buffered(1))
    return pl.pallas_call(
        functools.partial(_ffn_kernel, final_norm=final_norm),
        out_shape=jax.ShapeDtypeStruct((b, s, d), F32),
        grid=(b, s // tm),
        in_specs=[pl.BlockSpec((1, tm, d), tok),
                  pl.BlockSpec((1, FFN_HALO, d), prev),
                  pl.BlockSpec((1, FFN_HALO, d), nxt),
                  _full(g_ffn.shape),
                  pl.BlockSpec(w_up.shape, lambda *_: (0, 0), **single),
                  _full(w_dw.shape), _full(b_dw.shape),
                  pl.BlockSpec(w_down.shape, lambda *_: (0, 0), **single),
                  _full(g_final.shape)],
        out_specs=pl.BlockSpec((1, tm, d), tok),
        scratch_shapes=[pltpu.VMEM((tm + 2 * FFN_HALO, d), BF16),
                        pltpu.VMEM((tm + 2 * FFN_HALO, FF_CHUNK), F32),
                        pltpu.VMEM((tm + 2 * FFN_HALO, FF_CHUNK), F32)],
        compiler_params=_cparams(2),
        name="ffn_final" if final_norm else "ffn",
    )(x, x, x, g_ffn, w_up, w_dw, b_dw, w_down, g_final)


def _rot_cols(w):
    half = w.shape[-1] // 2
    return jnp.concatenate([-w[..., half:], w[..., :half]], axis=-1)


def _prep_layer(w_in, w_uq, w_ukv, w_o):
    o1 = Q_LORA
    o2 = o1 + KV_LORA
    o3 = o2 + D_ROPE
    w_kr = w_in[:, o2:o3]
    kr_block = jnp.concatenate(
        [jnp.zeros((D_MODEL, D_NOPE), F32), w_kr, _rot_cols(w_kr)], axis=1)
    w_in_ext = jnp.concatenate([w_in[:, :o2], w_in[:, o3:], kr_block], axis=1).astype(BF16)

    wq = w_uq.reshape(Q_LORA, N_HEADS, D_NOPE + D_ROPE)
    pad = jnp.zeros((Q_LORA, N_HEADS, HEAD_PAD - D_NOPE - D_ROPE), F32)
    q_main = jnp.concatenate([wq, pad], axis=2).reshape(Q_LORA, N_HEADS * HEAD_PAD)
    q_rot = jnp.concatenate(
        [jnp.zeros((Q_LORA, N_HEADS, D_NOPE), F32), _rot_cols(wq[:, :, D_NOPE:]), pad],
        axis=2).reshape(Q_LORA, N_HEADS * HEAD_PAD)
    w_uqt = jnp.concatenate([q_main, q_rot], axis=1).T.astype(BF16)

    wkv = w_ukv.reshape(KV_LORA, N_HEADS, D_NOPE + D_V)
    w_k = jnp.concatenate(
        [wkv[:, :, :D_NOPE], jnp.zeros((KV_LORA, N_HEADS, HEAD_PAD - D_NOPE), F32)],
        axis=2).reshape(KV_LORA, N_HEADS * HEAD_PAD).astype(BF16)
    w_vt = wkv[:, :, D_NOPE:].reshape(KV_LORA, N_HEADS * D_V).T.astype(BF16)

    w_oa = w_o[:N_HEADS * D_V].astype(BF16)
    w_oc = w_o[N_HEADS * D_V:].astype(BF16)
    return w_in_ext, w_uqt, w_k, w_vt, w_oa, w_oc


def kernel(x, positions, g_mix, w_in, g_q, w_uq, g_kv, w_ukv, w_dw_conv, b_dw_conv, g_conv_ln, b_conv_ln, w_o, g_ffn, w_up, w_dw_ffn, b_dw_ffn, w_down, g_final):
    depth = w_in.shape[0]
    cos_t, sin_t, tab_k = _rope_tables(positions)
    row = lambda v: v.reshape(1, -1)
    for l in range(depth):
        w_in_ext, w_uqt, w_k, w_vt, w_oa, w_oc = _prep_layer(w_in[l], w_uq[l], w_ukv[l], w_o[l])
        qt, k, vt, u = _proj(x, cos_t, sin_t, tab_k, row(g_mix[l]), w_in_ext, row(g_q[l]),
                             w_uqt, row(g_kv[l]), w_k, w_vt)
        yt = _attention(qt, k, vt)
        x = _mix(yt, u, x, w_dw_conv[l], row(b_dw_conv[l]), row(g_conv_ln[l]),
                 row(b_conv_ln[l]), w_oa, w_oc)
        x = _ffn(x, row(g_ffn[l]), w_up[l].astype(BF16), w_dw_ffn[l], row(b_dw_ffn[l]),
                 w_down[l].astype(BF16), row(g_final), final_norm=(l == depth - 1))
    return x
```

```python
import functools
import math

import jax
import jax.numpy as jnp
from jax import lax
from jax.experimental import pallas as pl
from jax.experimental.pallas import tpu as pltpu

D_MODEL = 1024
N_HEADS = 8
D_NOPE = 64
D_ROPE = 32
D_V = 64
Q_LORA = 384
KV_LORA = 256
CONV_CH = 512
CONV_K = 31
D_FF = 2816
FFN_K = 3
ROPE_THETA = 10000.0
EPS = 1e-6

HEAD_PAD = 128
VT_ROWS = 80
LOG2E = 1.4426950408889634
IN_EXT = Q_LORA + KV_LORA + 2 * CONV_CH + HEAD_PAD
CONV_HALO = 16
CONV_ROWS = 256
FFN_HALO = 8
FF_CHUNK = 256

TM_PROJ = 512
TQ = 256
TK = 256
TM_MIX = 512
TM_FFN = 512

VMEM_LIMIT = 56 * 1024 * 1024

F32 = jnp.float32
BF16 = jnp.bfloat16


def _cparams(n_axes):
    return pltpu.CompilerParams(
        dimension_semantics=("arbitrary",) * n_axes, vmem_limit_bytes=VMEM_LIMIT)


def _full(shape):
    return pl.BlockSpec(shape, lambda *_: (0,) * len(shape))


def _rms(v, g):
    return v * lax.rsqrt(jnp.mean(v * v, axis=-1, keepdims=True) + EPS) * g


def _table_kernel(pos_ref, invf_ref, cos_ref, sin_ref, tabk_ref):
    n_b = pos_ref.shape[0]
    s = pos_ref.shape[2]
    zeros = jnp.zeros((D_NOPE, s), F32)
    for b in range(n_b):
        ang = pos_ref[b].astype(F32) * invf_ref[...]
        c = jnp.cos(ang)
        sn = jnp.sin(ang)
        cos_ref[b] = c
        sin_ref[b] = sn
        tabk_ref[b] = jnp.concatenate([zeros, c, c, sn, sn], axis=0).T


def _rope_tables(positions):
    b, s = positions.shape
    half = D_ROPE // 2
    inv_freq = 1.0 / (ROPE_THETA ** (jnp.arange(0, D_ROPE, 2, dtype=F32) / D_ROPE))
    invf = jnp.broadcast_to(inv_freq[:, None], (half, s))
    return pl.pallas_call(
        _table_kernel,
        out_shape=(jax.ShapeDtypeStruct((b, half, s), F32),
                   jax.ShapeDtypeStruct((b, half, s), F32),
                   jax.ShapeDtypeStruct((b, s, HEAD_PAD), F32)),
        grid=(1,),
        in_specs=[_full((b, 1, s)), _full((half, s))],
        out_specs=(_full((b, half, s)), _full((b, half, s)), _full((b, s, HEAD_PAD))),
        compiler_params=_cparams(1),
        name="rope_tables",
    )(positions.reshape(b, 1, s), invf)


def _proj_kernel(x_ref, cos_ref, sin_ref, tabk_ref, gmix_ref, win_ref, gq_ref, wuqt_ref,
                 gkv_ref, wk_ref, wvt_ref, qt_ref, k_ref, vt_ref, u_ref):
    tm = x_ref.shape[1]
    scale = LOG2E / math.sqrt(D_NOPE + D_ROPE)
    h = _rms(x_ref[0], gmix_ref[...]).astype(BF16)
    p = jnp.dot(h, win_ref[...], preferred_element_type=F32)
    o1 = Q_LORA
    o2 = o1 + KV_LORA
    o3 = o2 + CONV_CH
    o4 = o3 + CONV_CH
    c_q = _rms(p[:, :o1], gq_ref[...]).astype(BF16)
    c_kv = _rms(p[:, o1:o2], gkv_ref[...]).astype(BF16)

    a = p[:, o2:o3]
    gt = p[:, o3:o4]
    u_ref[0] = a * jax.nn.sigmoid(gt)

    nt = (((1,), (1,)), ((), ()))
    q2 = lax.dot_general(wuqt_ref[...], c_q, nt, preferred_element_type=F32)
    c = cos_ref[0] * scale
    s = sin_ref[0] * scale
    cos_t = jnp.concatenate([jnp.full((D_NOPE, tm), scale, F32), c, c,
                             jnp.zeros((HEAD_PAD - D_NOPE - D_ROPE, tm), F32)], axis=0)
    sin_t = jnp.concatenate([jnp.zeros((D_NOPE, tm), F32), s, s,
                             jnp.zeros((HEAD_PAD - D_NOPE - D_ROPE, tm), F32)], axis=0)
    n_q = N_HEADS * HEAD_PAD
    for hh in range(N_HEADS):
        lo = hh * HEAD_PAD
        qh = q2[lo:lo + HEAD_PAD] * cos_t + q2[n_q + lo:n_q + lo + HEAD_PAD] * sin_t
        qt_ref[0, hh] = qh.astype(BF16)

    kn = jnp.dot(c_kv, wk_ref[...], preferred_element_type=F32)
    kr = p[:, o4:o4 + HEAD_PAD] * tabk_ref[0]
    kr = kr + pltpu.roll(kr, HEAD_PAD - D_ROPE, axis=1)
    lane = lax.broadcasted_iota(jnp.int32, kr.shape, 1)
    kr = jnp.where((lane >= D_NOPE) & (lane < D_NOPE + D_ROPE), kr, 0.0)
    for hh in range(N_HEADS):
        lo = hh * HEAD_PAD
        k_ref[0, hh] = (kn[:, lo:lo + HEAD_PAD] + kr).astype(BF16)

    vt = lax.dot_general(wvt_ref[...], c_kv, nt, preferred_element_type=F32)
    for hh in range(N_HEADS):
        vt_ref[0, hh, 0:D_V] = vt[hh * D_V:(hh + 1) * D_V].astype(BF16)
        vt_ref[0, hh, D_V:VT_ROWS] = jnp.ones((VT_ROWS - D_V, tm), BF16)


def _proj(x, cos_t, sin_t, tab_k, g_mix, w_in_ext, g_q, w_uqt, g_kv, w_k, w_vt):
    b, s, d = x.shape
    tm = TM_PROJ
    half = D_ROPE // 2
    tok = lambda bi, i: (bi, i, 0)
    tok_t = lambda bi, i: (bi, 0, i)
    return pl.pallas_call(
        _proj_kernel,
        out_shape=(jax.ShapeDtypeStruct((b, N_HEADS, HEAD_PAD, s), BF16),
                   jax.ShapeDtypeStruct((b, N_HEADS, s, HEAD_PAD), BF16),
                   jax.ShapeDtypeStruct((b, N_HEADS, VT_ROWS, s), BF16),
                   jax.ShapeDtypeStruct((b, s, CONV_CH), F32)),
        grid=(b, s // tm),
        in_specs=[pl.BlockSpec((1, tm, d), tok),
                  pl.BlockSpec((1, half, tm), tok_t),
                  pl.BlockSpec((1, half, tm), tok_t),
                  pl.BlockSpec((1, tm, HEAD_PAD), tok),
                  _full(g_mix.shape), _full(w_in_ext.shape), _full(g_q.shape),
                  _full(w_uqt.shape), _full(g_kv.shape), _full(w_k.shape), _full(w_vt.shape)],
        out_specs=(pl.BlockSpec((1, N_HEADS, HEAD_PAD, tm), lambda bi, i: (bi, 0, 0, i)),
                   pl.BlockSpec((1, N_HEADS, tm, HEAD_PAD), lambda bi, i: (bi, 0, i, 0)),
                   pl.BlockSpec((1, N_HEADS, VT_ROWS, tm), lambda bi, i: (bi, 0, 0, i)),
                   pl.BlockSpec((1, tm, CONV_CH), tok)),
        compiler_params=_cparams(2),
        name="proj",
    )(x, cos_t, sin_t, tab_k, g_mix, w_in_ext, g_q, w_uqt, g_kv, w_k, w_vt)


def _attn_kernel(qt_ref, k_ref, vt_ref, yt_ref, s_ref):
    tq = qt_ref.shape[3]
    s_len = k_ref.shape[2]
    n_chunks = s_len // TK
    groups = TK // 8

    def scores(hh, buf):
        qh = qt_ref[0, hh]
        m8 = None
        for c in range(n_chunks):
            kc = k_ref[0, hh, c * TK:(c + 1) * TK, :]
            st = jnp.dot(kc, qh, preferred_element_type=F32)
            s_ref[buf, c * TK:(c + 1) * TK, :] = st
            mc = jnp.max(st.reshape(groups, 8, tq), axis=0)
            m8 = mc if m8 is None else jnp.maximum(m8, mc)
        return jnp.max(m8, axis=0, keepdims=True)

    def weighted_values(hh, buf, m):
        acc = jnp.zeros((VT_ROWS, tq), F32)
        for c in range(n_chunks):
            pt = jnp.exp2(s_ref[buf, c * TK:(c + 1) * TK, :] - m)
            vc = vt_ref[0, hh, :, c * TK:(c + 1) * TK]
            acc = acc + jnp.dot(vc, pt.astype(BF16), preferred_element_type=F32)
        yt_ref[0, hh] = (acc[:D_V] / acc[D_V:D_V + 1]).astype(BF16)

    m = scores(0, 0)
    for hh in range(N_HEADS):
        m_next = scores(hh + 1, (hh + 1) % 2) if hh + 1 < N_HEADS else None
        weighted_values(hh, hh % 2, m)
        m = m_next


def _attention(qt, k, vt):
    b, nh, _, s = qt.shape
    return pl.pallas_call(
        _attn_kernel,
        out_shape=jax.ShapeDtypeStruct((b, nh, D_V, s), BF16),
        grid=(b, s // TQ),
        in_specs=[pl.BlockSpec((1, nh, HEAD_PAD, TQ), lambda bi, i: (bi, 0, 0, i)),
                  pl.BlockSpec((1, nh, s, HEAD_PAD), lambda bi, i: (bi, 0, 0, 0)),
                  pl.BlockSpec((1, nh, VT_ROWS, s), lambda bi, i: (bi, 0, 0, 0))],
        out_specs=pl.BlockSpec((1, nh, D_V, TQ), lambda bi, i: (bi, 0, 0, i)),
        scratch_shapes=[pltpu.VMEM((2, s, TQ), F32)],
        compiler_params=_cparams(2),
        name="attn",
    )(qt, k, vt)


def _mix_kernel(yt_ref, u_ref, up_ref, un_ref, x_ref, wdw_ref, bdw_ref, gln_ref, bln_ref,
                woa_ref, woc_ref, o_ref, sh_ref, conv_ref):
    tm = u_ref.shape[1]
    rows = tm + 2 * CONV_HALO
    i = pl.program_id(1)
    last = pl.num_programs(1) - 1
    sh_ref[0, 0:CONV_HALO] = jnp.where(i > 0, up_ref[0], 0.0)
    sh_ref[0, CONV_HALO:CONV_HALO + tm] = u_ref[0]
    sh_ref[0, CONV_HALO + tm:rows] = jnp.where(i < last, un_ref[0], 0.0)
    ext = sh_ref[0]
    for j in range(1, 8):
        sh_ref[j] = pltpu.roll(ext, rows - j, axis=0)

    base = CONV_HALO - CONV_K // 2
    for cb in range(CONV_CH // 128):
        lanes = slice(cb * 128, (cb + 1) * 128)
        for rb in range(tm // CONV_ROWS):
            r0 = rb * CONV_ROWS
            acc = jnp.broadcast_to(bdw_ref[:, lanes], (CONV_ROWS, 128))
            for kk in range(CONV_K):
                off = base + kk
                j = off % 8
                a0 = off - j + r0
                acc = acc + wdw_ref[kk:kk + 1, lanes] * sh_ref[j, a0:a0 + CONV_ROWS, lanes]
            conv_ref[r0:r0 + CONV_ROWS, lanes] = acc
    acc = conv_ref[...]

    mu = jnp.mean(acc, axis=-1, keepdims=True)
    cen = acc - mu
    var = jnp.mean(cen * cen, axis=-1, keepdims=True)
    y = cen * lax.rsqrt(var + EPS) * gln_ref[...] + bln_ref[...]
    conv = (y * jax.nn.sigmoid(y)).astype(BF16)

    yt = yt_ref[0].reshape(N_HEADS * D_V, tm)
    tn = (((0,), (0,)), ((), ()))
    out = lax.dot_general(yt, woa_ref[...], tn, preferred_element_type=F32)
    out = out + jnp.dot(conv, woc_ref[...], preferred_element_type=F32)
    o_ref[0] = x_ref[0] + out


def _mix(yt, u, x, w_dw, b_dw, g_ln, b_ln, w_oa, w_oc):
    b, s, d = x.shape
    tm = TM_MIX
    r = tm // CONV_HALO
    n_halo = s // CONV_HALO
    tok = lambda bi, i: (bi, i, 0)
    prev = lambda bi, i: (bi, jnp.maximum(i * r - 1, 0), 0)
    nxt = lambda bi, i: (bi, jnp.minimum((i + 1) * r, n_halo - 1), 0)
    return pl.pallas_call(
        _mix_kernel,
        out_shape=jax.ShapeDtypeStruct((b, s, d), F32),
        grid=(b, s // tm),
        in_specs=[pl.BlockSpec((1, N_HEADS, D_V, tm), lambda bi, i: (bi, 0, 0, i)),
                  pl.BlockSpec((1, tm, CONV_CH), tok),
                  pl.BlockSpec((1, CONV_HALO, CONV_CH), prev),
                  pl.BlockSpec((1, CONV_HALO, CONV_CH), nxt),
                  pl.BlockSpec((1, tm, d), tok),
                  _full(w_dw.shape), _full(b_dw.shape), _full(g_ln.shape), _full(b_ln.shape),
                  _full(w_oa.shape), _full(w_oc.shape)],
        out_specs=pl.BlockSpec((1, tm, d), tok),
        scratch_shapes=[pltpu.VMEM((8, tm + 2 * CONV_HALO, CONV_CH), F32),
                        pltpu.VMEM((tm, CONV_CH), F32)],
        compiler_params=_cparams(2),
        name="mix",
    )(yt, u, u, u, x, w_dw, b_dw, g_ln, b_ln, w_oa, w_oc)


def _ffn_kernel(x_ref, xp_ref, xn_ref, g_ref, wup_ref, wdw_ref, bdw_ref, wdn_ref, gfin_ref,
                o_ref, h_ref, *, final_norm):
    tm = x_ref.shape[1]
    i = pl.program_id(1)
    last = pl.num_programs(1) - 1
    g = g_ref[...]
    hp = jnp.where(i > 0, _rms(xp_ref[0], g), 0.0)
    hn = jnp.where(i < last, _rms(xn_ref[0], g), 0.0)
    h_ref[0:FFN_HALO] = hp.astype(BF16)
    h_ref[FFN_HALO:FFN_HALO + tm] = _rms(x_ref[0], g).astype(BF16)
    h_ref[FFN_HALO + tm:FFN_HALO + tm + FFN_HALO] = hn.astype(BF16)
    h = h_ref[...]

    rows = tm + 2 * FFN_HALO

    def conv3(z, col):
        w = wdw_ref[:, col:col + FF_CHUNK]
        lo = FFN_HALO
        prev_rows = pltpu.roll(z, 1, axis=0)[lo:lo + tm]
        next_rows = pltpu.roll(z, rows - 1, axis=0)[lo:lo + tm]
        return (bdw_ref[:, col:col + FF_CHUNK] + w[0:1] * prev_rows + w[1:2] * z[lo:lo + tm]
                + w[2:3] * next_rows)

    acc = jnp.zeros((tm, D_MODEL), F32)
    for c in range(D_FF // FF_CHUNK):
        cg = c * FF_CHUNK
        cv = D_FF + c * FF_CHUNK
        zg = conv3(jnp.dot(h, wup_ref[:, cg:cg + FF_CHUNK], preferred_element_type=F32), cg)
        zv = conv3(jnp.dot(h, wup_ref[:, cv:cv + FF_CHUNK], preferred_element_type=F32), cv)
        act = (zg * jax.nn.sigmoid(zg) * zv).astype(BF16)
        acc = acc + jnp.dot(act, wdn_ref[cg:cg + FF_CHUNK, :], preferred_element_type=F32)

    out = x_ref[0] + acc
    if final_norm:
        out = _rms(out, gfin_ref[...])
    o_ref[0] = out


def _ffn(x, g_ffn, w_up, w_dw, b_dw, w_down, g_final, final_norm):
    b, s, d = x.shape
    tm = TM_FFN
    r = tm // FFN_HALO
    n_halo = s // FFN_HALO
    tok = lambda bi, i: (bi, i, 0)
    prev = lambda bi, i: (bi, jnp.maximum(i * r - 1, 0), 0)
    nxt = lambda bi, i: (bi, jnp.minimum((i + 1) * r, n_halo - 1), 0)
    single = dict(pipeline_mode=pl.Buffered(1))
    return pl.pallas_call(
        functools.partial(_ffn_kernel, final_norm=final_norm),
        out_shape=jax.ShapeDtypeStruct((b, s, d), F32),
        grid=(b, s // tm),
        in_specs=[pl.BlockSpec((1, tm, d), tok),
                  pl.BlockSpec((1, FFN_HALO, d), prev),
                  pl.BlockSpec((1, FFN_HALO, d), nxt),
                  _full(g_ffn.shape),
                  pl.BlockSpec(w_up.shape, lambda *_: (0, 0), **single),
                  _full(w_dw.shape), _full(b_dw.shape),
                  pl.BlockSpec(w_down.shape, lambda *_: (0, 0), **single),
                  _full(g_final.shape)],
        out_specs=pl.BlockSpec((1, tm, d), tok),
        scratch_shapes=[pltpu.VMEM((tm + 2 * FFN_HALO, d), BF16)],
        compiler_params=_cparams(2),
        name="ffn_final" if final_norm else "ffn",
    )(x, x, x, g_ffn, w_up, w_dw, b_dw, w_down, g_final)


def _rot_cols(w):
    half = w.shape[-1] // 2
    return jnp.concatenate([-w[..., half:], w[..., :half]], axis=-1)


def _prep_layer(w_in, w_uq, w_ukv, w_o):
    o1 = Q_LORA
    o2 = o1 + KV_LORA
    o3 = o2 + D_ROPE
    w_kr = w_in[:, o2:o3]
    kr_block = jnp.concatenate(
        [jnp.zeros((D_MODEL, D_NOPE), F32), w_kr, _rot_cols(w_kr)], axis=1)
    w_in_ext = jnp.concatenate([w_in[:, :o2], w_in[:, o3:], kr_block], axis=1).astype(BF16)

    wq = w_uq.reshape(Q_LORA, N_HEADS, D_NOPE + D_ROPE)
    pad = jnp.zeros((Q_LORA, N_HEADS, HEAD_PAD - D_NOPE - D_ROPE), F32)
    q_main = jnp.concatenate([wq, pad], axis=2).reshape(Q_LORA, N_HEADS * HEAD_PAD)
    q_rot = jnp.concatenate(
        [jnp.zeros((Q_LORA, N_HEADS, D_NOPE), F32), _rot_cols(wq[:, :, D_NOPE:]), pad],
        axis=2).reshape(Q_LORA, N_HEADS * HEAD_PAD)
    w_uqt = jnp.concatenate([q_main, q_rot], axis=1).T.astype(BF16)

    wkv = w_ukv.reshape(KV_LORA, N_HEADS, D_NOPE + D_V)
    w_k = jnp.concatenate(
        [wkv[:, :, :D_NOPE], jnp.zeros((KV_LORA, N_HEADS, HEAD_PAD - D_NOPE), F32)],
        axis=2).reshape(KV_LORA, N_HEADS * HEAD_PAD).astype(BF16)
    w_vt = wkv[:, :, D_NOPE:].reshape(KV_LORA, N_HEADS * D_V).T.astype(BF16)

    w_oa = w_o[:N_HEADS * D_V].astype(BF16)
    w_oc = w_o[N_HEADS * D_V:].astype(BF16)
    return w_in_ext, w_uqt, w_k, w_vt, w_oa, w_oc


def kernel(x, positions, g_mix, w_in, g_q, w_uq, g_kv, w_ukv, w_dw_conv, b_dw_conv, g_conv_ln, b_conv_ln, w_o, g_ffn, w_up, w_dw_ffn, b_dw_ffn, w_down, g_final):
    depth = w_in.shape[0]
    cos_t, sin_t, tab_k = _rope_tables(positions)
    row = lambda v: v.reshape(1, -1)
    for l in range(depth):
        w_in_ext, w_uqt, w_k, w_vt, w_oa, w_oc = _prep_layer(w_in[l], w_uq[l], w_ukv[l], w_o[l])
        qt, k, vt, u = _proj(x, cos_t, sin_t, tab_k, row(g_mix[l]), w_in_ext, row(g_q[l]),
                             w_uqt, row(g_kv[l]), w_k, w_vt)
        yt = _attention(qt, k, vt)
        x = _mix(yt, u, x, w_dw_conv[l], row(b_dw_conv[l]), row(g_conv_ln[l]),
                 row(b_conv_ln[l]), w_oa, w_oc)
        x = _ffn(x, row(g_ffn[l]), w_up[l].astype(BF16), w_dw_ffn[l], row(b_dw_ffn[l]),
                 w_down[l].astype(BF16), row(g_final), final_norm=(l == depth - 1))
    return x
```

```python
import functools
import math

import jax
import jax.numpy as jnp
from jax import lax
from jax.experimental import pallas as pl
from jax.experimental.pallas import tpu as pltpu

D_MODEL = 1024
N_HEADS = 8
D_NOPE = 64
D_ROPE = 32
D_V = 64
Q_LORA = 384
KV_LORA = 256
CONV_CH = 512
CONV_K = 31
D_FF = 2816
FFN_K = 3
ROPE_THETA = 10000.0
EPS = 1e-6

HEAD_PAD = 128
VT_ROWS = 80
LOG2E = 1.4426950408889634
IN_EXT = Q_LORA + KV_LORA + 2 * CONV_CH + HEAD_PAD
CONV_HALO = 16
CONV_ROWS = 256
FFN_HALO = 8
FF_CHUNK = 1536

TM_PROJ = 512
TQ = 256
TK = 256
TM_MIX = 512
TM_FFN = 512

VMEM_LIMIT = 56 * 1024 * 1024

F32 = jnp.float32
BF16 = jnp.bfloat16


def _cparams(n_axes):
    return pltpu.CompilerParams(
        dimension_semantics=("arbitrary",) * n_axes, vmem_limit_bytes=VMEM_LIMIT)


def _full(shape):
    return pl.BlockSpec(shape, lambda *_: (0,) * len(shape))


def _rms(v, g):
    return v * lax.rsqrt(jnp.mean(v * v, axis=-1, keepdims=True) + EPS) * g


def _table_kernel(pos_ref, invf_ref, cos_ref, sin_ref, tabk_ref):
    n_b = pos_ref.shape[0]
    s = pos_ref.shape[2]
    zeros = jnp.zeros((D_NOPE, s), F32)
    for b in range(n_b):
        ang = pos_ref[b].astype(F32) * invf_ref[...]
        c = jnp.cos(ang)
        sn = jnp.sin(ang)
        cos_ref[b] = c
        sin_ref[b] = sn
        tabk_ref[b] = jnp.concatenate([zeros, c, c, sn, sn], axis=0).T


def _rope_tables(positions):
    b, s = positions.shape
    half = D_ROPE // 2
    inv_freq = 1.0 / (ROPE_THETA ** (jnp.arange(0, D_ROPE, 2, dtype=F32) / D_ROPE))
    invf = jnp.broadcast_to(inv_freq[:, None], (half, s))
    return pl.pallas_call(
        _table_kernel,
        out_shape=(jax.ShapeDtypeStruct((b, half, s), F32),
                   jax.ShapeDtypeStruct((b, half, s), F32),
                   jax.ShapeDtypeStruct((b, s, HEAD_PAD), F32)),
        grid=(1,),
        in_specs=[_full((b, 1, s)), _full((half, s))],
        out_specs=(_full((b, half, s)), _full((b, half, s)), _full((b, s, HEAD_PAD))),
        compiler_params=_cparams(1),
        name="rope_tables",
    )(positions.reshape(b, 1, s), invf)


def _proj_kernel(x_ref, cos_ref, sin_ref, tabk_ref, gmix_ref, win_ref, gq_ref, wuqt_ref,
                 gkv_ref, wk_ref, wvt_ref, qt_ref, k_ref, vt_ref, u_ref):
    tm = x_ref.shape[1]
    scale = LOG2E / math.sqrt(D_NOPE + D_ROPE)
    h = _rms(x_ref[0], gmix_ref[...]).astype(BF16)
    p = jnp.dot(h, win_ref[...], preferred_element_type=F32)
    o1 = Q_LORA
    o2 = o1 + KV_LORA
    o3 = o2 + CONV_CH
    o4 = o3 + CONV_CH
    c_q = _rms(p[:, :o1], gq_ref[...]).astype(BF16)
    c_kv = _rms(p[:, o1:o2], gkv_ref[...]).astype(BF16)

    a = p[:, o2:o3]
    gt = p[:, o3:o4]
    u_ref[0] = a * jax.nn.sigmoid(gt)

    nt = (((1,), (1,)), ((), ()))
    q2 = lax.dot_general(wuqt_ref[...], c_q, nt, preferred_element_type=F32)
    c = cos_ref[0] * scale
    s = sin_ref[0] * scale
    cos2 = jnp.concatenate([c, c], axis=0)
    sin2 = jnp.concatenate([s, s], axis=0)
    o_rope = N_HEADS * D_NOPE
    o_rot = o_rope + N_HEADS * D_ROPE
    zpad = jnp.zeros((HEAD_PAD - D_NOPE - D_ROPE, tm), F32)
    for hh in range(N_HEADS):
        nope = q2[hh * D_NOPE:(hh + 1) * D_NOPE] * scale
        rope = (q2[o_rope + hh * D_ROPE:o_rope + (hh + 1) * D_ROPE] * cos2
                + q2[o_rot + hh * D_ROPE:o_rot + (hh + 1) * D_ROPE] * sin2)
        qt_ref[0, hh] = jnp.concatenate([nope, rope, zpad], axis=0).astype(BF16)

    kn = jnp.dot(c_kv, wk_ref[...], preferred_element_type=F32)
    kr = p[:, o4:o4 + HEAD_PAD] * tabk_ref[0]
    kr = kr + pltpu.roll(kr, HEAD_PAD - D_ROPE, axis=1)
    lane = lax.broadcasted_iota(jnp.int32, kr.shape, 1)
    kr = jnp.where((lane >= D_NOPE) & (lane < D_NOPE + D_ROPE), kr, 0.0)
    for hh in range(N_HEADS):
        lo = hh * HEAD_PAD
        k_ref[0, hh] = (kn[:, lo:lo + HEAD_PAD] + kr).astype(BF16)

    vt = lax.dot_general(wvt_ref[...], c_kv, nt, preferred_element_type=F32)
    for hh in range(N_HEADS):
        vt_ref[0, hh, 0:D_V] = vt[hh * D_V:(hh + 1) * D_V].astype(BF16)
        vt_ref[0, hh, D_V:VT_ROWS] = jnp.ones((VT_ROWS - D_V, tm), BF16)


def _proj(x, cos_t, sin_t, tab_k, g_mix, w_in_ext, g_q, w_uqt, g_kv, w_k, w_vt):
    b, s, d = x.shape
    tm = TM_PROJ
    half = D_ROPE // 2
    tok = lambda bi, i: (bi, i, 0)
    tok_t = lambda bi, i: (bi, 0, i)
    return pl.pallas_call(
        _proj_kernel,
        out_shape=(jax.ShapeDtypeStruct((b, N_HEADS, HEAD_PAD, s), BF16),
                   jax.ShapeDtypeStruct((b, N_HEADS, s, HEAD_PAD), BF16),
                   jax.ShapeDtypeStruct((b, N_HEADS, VT_ROWS, s), BF16),
                   jax.ShapeDtypeStruct((b, s, CONV_CH), F32)),
        grid=(b, s // tm),
        in_specs=[pl.BlockSpec((1, tm, d), tok),
                  pl.BlockSpec((1, half, tm), tok_t),
                  pl.BlockSpec((1, half, tm), tok_t),
                  pl.BlockSpec((1, tm, HEAD_PAD), tok),
                  _full(g_mix.shape), _full(w_in_ext.shape), _full(g_q.shape),
                  _full(w_uqt.shape), _full(g_kv.shape), _full(w_k.shape), _full(w_vt.shape)],
        out_specs=(pl.BlockSpec((1, N_HEADS, HEAD_PAD, tm), lambda bi, i: (bi, 0, 0, i)),
                   pl.BlockSpec((1, N_HEADS, tm, HEAD_PAD), lambda bi, i: (bi, 0, i, 0)),
                   pl.BlockSpec((1, N_HEADS, VT_ROWS, tm), lambda bi, i: (bi, 0, 0, i)),
                   pl.BlockSpec((1, tm, CONV_CH), tok)),
        compiler_params=_cparams(2),
        name="proj",
    )(x, cos_t, sin_t, tab_k, g_mix, w_in_ext, g_q, w_uqt, g_kv, w_k, w_vt)


def _attn_kernel(qt_ref, k_ref, vt_ref, yt_ref, s0_ref, s1_ref):
    s_bufs = (s0_ref, s1_ref)
    dyn0 = jnp.minimum(pl.program_id(0), 0)
    tq = qt_ref.shape[3]
    s_len = k_ref.shape[2]
    n_chunks = s_len // TK
    groups = TK // 8

    def scores(hh, buf):
        qh = qt_ref[0, hh]
        m8 = None
        for c in range(n_chunks):
            kc = k_ref[0, hh, c * TK:(c + 1) * TK, :]
            st = jnp.dot(kc, qh, preferred_element_type=F32)
            s_bufs[buf][c * TK:(c + 1) * TK, :] = st
            mc = jnp.max(st.reshape(groups, 8, tq), axis=0)
            m8 = mc if m8 is None else jnp.maximum(m8, mc)
        return jnp.max(m8, axis=0, keepdims=True)

    def weighted_values(hh, buf, m):
        acc = jnp.zeros((VT_ROWS, tq), F32)
        for c in range(n_chunks):
            row0 = pl.multiple_of(dyn0 + c * TK, TK)
            pt = jnp.exp2(s_bufs[buf][pl.ds(row0, TK), :] - m)
            vc = vt_ref[0, hh, :, c * TK:(c + 1) * TK]
            acc = acc + jnp.dot(vc, pt.astype(BF16), preferred_element_type=F32)
        yt_ref[0, hh] = (acc[:D_V] / acc[D_V:D_V + 1]).astype(BF16)

    m = scores(0, 0)
    for hh in range(N_HEADS):
        m_next = scores(hh + 1, (hh + 1) % 2) if hh + 1 < N_HEADS else None
        weighted_values(hh, hh % 2, m)
        m = m_next


def _attention(qt, k, vt):
    b, nh, _, s = qt.shape
    return pl.pallas_call(
        _attn_kernel,
        out_shape=jax.ShapeDtypeStruct((b, nh, D_V, s), BF16),
        grid=(b, s // TQ),
        in_specs=[pl.BlockSpec((1, nh, HEAD_PAD, TQ), lambda bi, i: (bi, 0, 0, i)),
                  pl.BlockSpec((1, nh, s, HEAD_PAD), lambda bi, i: (bi, 0, 0, 0)),
                  pl.BlockSpec((1, nh, VT_ROWS, s), lambda bi, i: (bi, 0, 0, 0))],
        out_specs=pl.BlockSpec((1, nh, D_V, TQ), lambda bi, i: (bi, 0, 0, i)),
        scratch_shapes=[pltpu.VMEM((s, TQ), F32), pltpu.VMEM((s, TQ), F32)],
        compiler_params=_cparams(2),
        name="attn",
    )(qt, k, vt)


def _mix_kernel(yt_ref, u_ref, up_ref, un_ref, x_ref, wdw_ref, bdw_ref, gln_ref, bln_ref,
                woa_ref, woc_ref, o_ref, sh_ref, conv_ref):
    tm = u_ref.shape[1]
    rows = tm + 2 * CONV_HALO
    i = pl.program_id(1)
    last = pl.num_programs(1) - 1
    sh_ref[0, 0:CONV_HALO] = jnp.where(i > 0, up_ref[0], 0.0)
    sh_ref[0, CONV_HALO:CONV_HALO + tm] = u_ref[0]
    sh_ref[0, CONV_HALO + tm:rows] = jnp.where(i < last, un_ref[0], 0.0)
    ext = sh_ref[0]
    for j in range(1, 8):
        sh_ref[j] = pltpu.roll(ext, rows - j, axis=0)

    base = CONV_HALO - CONV_K // 2
    for cb in range(CONV_CH // 128):
        lanes = slice(cb * 128, (cb + 1) * 128)
        for rb in range(tm // CONV_ROWS):
            r0 = rb * CONV_ROWS
            acc = jnp.broadcast_to(bdw_ref[:, lanes], (CONV_ROWS, 128))
            for kk in range(CONV_K):
                off = base + kk
                j = off % 8
                a0 = off - j + r0
                acc = acc + wdw_ref[kk:kk + 1, lanes] * sh_ref[j, a0:a0 + CONV_ROWS, lanes]
            conv_ref[r0:r0 + CONV_ROWS, lanes] = acc
    acc = conv_ref[...]

    mu = jnp.mean(acc, axis=-1, keepdims=True)
    cen = acc - mu
    var = jnp.mean(cen * cen, axis=-1, keepdims=True)
    y = cen * lax.rsqrt(var + EPS) * gln_ref[...] + bln_ref[...]
    conv = (y * jax.nn.sigmoid(y)).astype(BF16)

    yt = yt_ref[0].reshape(N_HEADS * D_V, tm)
    tn = (((0,), (0,)), ((), ()))
    out = lax.dot_general(yt, woa_ref[...], tn, preferred_element_type=F32)
    out = out + jnp.dot(conv, woc_ref[...], preferred_element_type=F32)
    o_ref[0] = x_ref[0] + out


def _mix(yt, u, x, w_dw, b_dw, g_ln, b_ln, w_oa, w_oc):
    b, s, d = x.shape
    tm = TM_MIX
    r = tm // CONV_HALO
    n_halo = s // CONV_HALO
    tok = lambda bi, i: (bi, i, 0)
    prev = lambda bi, i: (bi, jnp.maximum(i * r - 1, 0), 0)
    nxt = lambda bi, i: (bi, jnp.minimum((i + 1) * r, n_halo - 1), 0)
    return pl.pallas_call(
        _mix_kernel,
        out_shape=jax.ShapeDtypeStruct((b, s, d), F32),
        grid=(b, s // tm),
        in_specs=[pl.BlockSpec((1, N_HEADS, D_V, tm), lambda bi, i: (bi, 0, 0, i)),
                  pl.BlockSpec((1, tm, CONV_CH), tok),
                  pl.BlockSpec((1, CONV_HALO, CONV_CH), prev),
                  pl.BlockSpec((1, CONV_HALO, CONV_CH), nxt),
                  pl.BlockSpec((1, tm, d), tok),
                  _full(w_dw.shape), _full(b_dw.shape), _full(g_ln.shape), _full(b_ln.shape),
                  _full(w_oa.shape), _full(w_oc.shape)],
        out_specs=pl.BlockSpec((1, tm, d), tok),
        scratch_shapes=[pltpu.VMEM((8, tm + 2 * CONV_HALO, CONV_CH), F32),
                        pltpu.VMEM((tm, CONV_CH), F32)],
        compiler_params=_cparams(2),
        name="mix",
    )(yt, u, u, u, x, w_dw, b_dw, g_ln, b_ln, w_oa, w_oc)


def _ffn_kernel(x_ref, xp_ref, xn_ref, g_ref, wup_ref, wdw_ref, bdw_ref, wdn_ref, gfin_ref,
                o_ref, h_ref, *, final_norm):
    tm = x_ref.shape[1]
    i = pl.program_id(1)
    last = pl.num_programs(1) - 1
    g = g_ref[...]
    hp = jnp.where(i > 0, _rms(xp_ref[0], g), 0.0)
    hn = jnp.where(i < last, _rms(xn_ref[0], g), 0.0)
    h_ref[0:FFN_HALO] = hp.astype(BF16)
    h_ref[FFN_HALO:FFN_HALO + tm] = _rms(x_ref[0], g).astype(BF16)
    h_ref[FFN_HALO + tm:FFN_HALO + tm + FFN_HALO] = hn.astype(BF16)
    h = h_ref[...]

    rows = tm + 2 * FFN_HALO

    def conv3(z, col, cw):
        w = wdw_ref[:, col:col + cw]
        lo = FFN_HALO
        prev_rows = pltpu.roll(z, 1, axis=0)[lo:lo + tm]
        next_rows = pltpu.roll(z, rows - 1, axis=0)[lo:lo + tm]
        return (bdw_ref[:, col:col + cw] + w[0:1] * prev_rows + w[1:2] * z[lo:lo + tm]
                + w[2:3] * next_rows)

    acc = jnp.zeros((tm, D_MODEL), F32)
    cg = 0
    while cg < D_FF:
        cw = min(FF_CHUNK, D_FF - cg)
        cv = D_FF + cg
        zg = conv3(jnp.dot(h, wup_ref[:, cg:cg + cw], preferred_element_type=F32), cg, cw)
        zv = conv3(jnp.dot(h, wup_ref[:, cv:cv + cw], preferred_element_type=F32), cv, cw)
        act = (zg * jax.nn.sigmoid(zg) * zv).astype(BF16)
        acc = acc + jnp.dot(act, wdn_ref[cg:cg + cw, :], preferred_element_type=F32)
        cg += cw

    out = x_ref[0] + acc
    if final_norm:
        out = _rms(out, gfin_ref[...])
    o_ref[0] = out


def _ffn(x, g_ffn, w_up, w_dw, b_dw, w_down, g_final, final_norm):
    b, s, d = x.shape
    tm = TM_FFN
    r = tm // FFN_HALO
    n_halo = s // FFN_HALO
    tok = lambda bi, i: (bi, i, 0)
    prev = lambda bi, i: (bi, jnp.maximum(i * r - 1, 0), 0)
    nxt = lambda bi, i: (bi, jnp.minimum((i + 1) * r, n_halo - 1), 0)
    single = dict(pipeline_mode=pl.Buffered(1))
    return pl.pallas_call(
        functools.partial(_ffn_kernel, final_norm=final_norm),
        out_shape=jax.ShapeDtypeStruct((b, s, d), F32),
        grid=(b, s // tm),
        in_specs=[pl.BlockSpec((1, tm, d), tok),
                  pl.BlockSpec((1, FFN_HALO, d), prev),
                  pl.BlockSpec((1, FFN_HALO, d), nxt),
                  _full(g_ffn.shape),
                  pl.BlockSpec(w_up.shape, lambda *_: (0, 0), **single),
                  _full(w_dw.shape), _full(b_dw.shape),
                  pl.BlockSpec(w_down.shape, lambda *_: (0, 0), **single),
                  _full(g_final.shape)],
        out_specs=pl.BlockSpec((1, tm, d), tok),
        scratch_shapes=[pltpu.VMEM((tm + 2 * FFN_HALO, d), BF16)],
        compiler_params=_cparams(2),
        name="ffn_final" if final_norm else "ffn",
    )(x, x, x, g_ffn, w_up, w_dw, b_dw, w_down, g_final)


def _rot_cols(w):
    half = w.shape[-1] // 2
    return jnp.concatenate([-w[..., half:], w[..., :half]], axis=-1)


def _prep_layer(w_in, w_uq, w_ukv, w_o):
    o1 = Q_LORA
    o2 = o1 + KV_LORA
    o3 = o2 + D_ROPE
    w_kr = w_in[:, o2:o3]
    kr_block = jnp.concatenate(
        [jnp.zeros((D_MODEL, D_NOPE), F32), w_kr, _rot_cols(w_kr)], axis=1)
    w_in_ext = jnp.concatenate([w_in[:, :o2], w_in[:, o3:], kr_block], axis=1).astype(BF16)

    wq = w_uq.reshape(Q_LORA, N_HEADS, D_NOPE + D_ROPE)
    q_nope = wq[:, :, :D_NOPE].reshape(Q_LORA, N_HEADS * D_NOPE)
    q_rope = wq[:, :, D_NOPE:].reshape(Q_LORA, N_HEADS * D_ROPE)
    q_rot = _rot_cols(wq[:, :, D_NOPE:]).reshape(Q_LORA, N_HEADS * D_ROPE)
    w_uqt = jnp.concatenate([q_nope, q_rope, q_rot], axis=1).T.astype(BF16)

    wkv = w_ukv.reshape(KV_LORA, N_HEADS, D_NOPE + D_V)
    w_k = jnp.concatenate(
        [wkv[:, :, :D_NOPE], jnp.zeros((KV_LORA, N_HEADS, HEAD_PAD - D_NOPE), F32)],
        axis=2).reshape(KV_LORA, N_HEADS * HEAD_PAD).astype(BF16)
    w_vt = wkv[:, :, D_NOPE:].reshape(KV_LORA, N_HEADS * D_V).T.astype(BF16)

    w_oa = w_o[:N_HEADS * D_V].astype(BF16)
    w_oc = w_o[N_HEADS * D_V:].astype(BF16)
    return w_in_ext, w_uqt, w_k, w_vt, w_oa, w_oc


def kernel(x, positions, g_mix, w_in, g_q, w_uq, g_kv, w_ukv, w_dw_conv, b_dw_conv, g_conv_ln, b_conv_ln, w_o, g_ffn, w_up, w_dw_ffn, b_dw_ffn, w_down, g_final):
    depth = w_in.shape[0]
    cos_t, sin_t, tab_k = _rope_tables(positions)
    row = lambda v: v.reshape(1, -1)
    for l in range(depth):
        w_in_ext, w_uqt, w_k, w_vt, w_oa, w_oc = _prep_layer(w_in[l], w_uq[l], w_ukv[l], w_o[l])
        qt, k, vt, u = _proj(x, cos_t, sin_t, tab_k, row(g_mix[l]), w_in_ext, row(g_q[l]),
                             w_uqt, row(g_kv[l]), w_k, w_vt)
        yt = _attention(qt, k, vt)
        x = _mix(yt, u, x, w_dw_conv[l], row(b_dw_conv[l]), row(g_conv_ln[l]),
                 row(b_conv_ln[l]), w_oa, w_oc)
        x = _ffn(x, row(g_ffn[l]), w_up[l].astype(BF16), w_dw_ffn[l], row(b_dw_ffn[l]),
                 w_down[l].astype(BF16), row(g_final), final_norm=(l == depth - 1))
    return x
```

```python
import functools
import math

import jax
import jax.numpy as jnp
from jax import lax
from jax.experimental import pallas as pl
from jax.experimental.pallas import tpu as pltpu

D_MODEL = 1024
N_HEADS = 8
D_NOPE = 64
D_ROPE = 32
D_V = 64
Q_LORA = 384
KV_LORA = 256
CONV_CH = 512
CONV_K = 31
D_FF = 2816
FFN_K = 3
ROPE_THETA = 10000.0
EPS = 1e-6

HEAD_PAD = 128
VT_ROWS = 80
LOG2E = 1.4426950408889634
IN_EXT = Q_LORA + KV_LORA + 2 * CONV_CH + HEAD_PAD
CONV_HALO = 16
CONV_ROWS = 256
FFN_HALO = 8
FF_CHUNK = 1536

TM_PROJ = 512
TQ = 512
TQ_SUB = 256
TK = 256
TM_MIX = 512
TM_FFN = 512

VMEM_LIMIT = 56 * 1024 * 1024

F32 = jnp.float32
BF16 = jnp.bfloat16


def _cparams(n_axes):
    return pltpu.CompilerParams(
        dimension_semantics=("arbitrary",) * n_axes, vmem_limit_bytes=VMEM_LIMIT)


def _full(shape):
    return pl.BlockSpec(shape, lambda *_: (0,) * len(shape))


def _rms(v, g):
    return v * lax.rsqrt(jnp.mean(v * v, axis=-1, keepdims=True) + EPS) * g


def _table_kernel(pos_ref, invf_ref, cos_ref, sin_ref, tabk_ref):
    n_b = pos_ref.shape[0]
    s = pos_ref.shape[2]
    zeros = jnp.zeros((D_NOPE, s), F32)
    for b in range(n_b):
        ang = pos_ref[b].astype(F32) * invf_ref[...]
        c = jnp.cos(ang)
        sn = jnp.sin(ang)
        cos_ref[b] = c
        sin_ref[b] = sn
        tabk_ref[b] = jnp.concatenate([zeros, c, c, sn, sn], axis=0).T


def _rope_tables(positions):
    b, s = positions.shape
    half = D_ROPE // 2
    inv_freq = 1.0 / (ROPE_THETA ** (jnp.arange(0, D_ROPE, 2, dtype=F32) / D_ROPE))
    invf = jnp.broadcast_to(inv_freq[:, None], (half, s))
    return pl.pallas_call(
        _table_kernel,
        out_shape=(jax.ShapeDtypeStruct((b, half, s), F32),
                   jax.ShapeDtypeStruct((b, half, s), F32),
                   jax.ShapeDtypeStruct((b, s, HEAD_PAD), F32)),
        grid=(1,),
        in_specs=[_full((b, 1, s)), _full((half, s))],
        out_specs=(_full((b, half, s)), _full((b, half, s)), _full((b, s, HEAD_PAD))),
        compiler_params=_cparams(1),
        name="rope_tables",
    )(positions.reshape(b, 1, s), invf)


def _proj_kernel(x_ref, cos_ref, sin_ref, tabk_ref, gmix_ref, win_ref, gq_ref, wuqt_ref,
                 gkv_ref, wk_ref, wvt_ref, qt_ref, k_ref, vt_ref, u_ref):
    tm = x_ref.shape[1]
    scale = LOG2E / math.sqrt(D_NOPE + D_ROPE)
    h = _rms(x_ref[0], gmix_ref[...]).astype(BF16)
    p = jnp.dot(h, win_ref[...], preferred_element_type=F32)
    o1 = Q_LORA
    o2 = o1 + KV_LORA
    o3 = o2 + CONV_CH
    o4 = o3 + CONV_CH
    c_q = _rms(p[:, :o1], gq_ref[...]).astype(BF16)
    c_kv = _rms(p[:, o1:o2], gkv_ref[...]).astype(BF16)

    a = p[:, o2:o3]
    gt = p[:, o3:o4]
    u_ref[0] = a * jax.nn.sigmoid(gt)

    nt = (((1,), (1,)), ((), ()))
    q2 = lax.dot_general(wuqt_ref[...], c_q, nt, preferred_element_type=F32)
    c = cos_ref[0] * scale
    s = sin_ref[0] * scale
    cos2 = jnp.concatenate([c, c], axis=0)
    sin2 = jnp.concatenate([s, s], axis=0)
    o_rope = N_HEADS * D_NOPE
    o_rot = o_rope + N_HEADS * D_ROPE
    zpad = jnp.zeros((HEAD_PAD - D_NOPE - D_ROPE, tm), F32)
    for hh in range(N_HEADS):
        nope = q2[hh * D_NOPE:(hh + 1) * D_NOPE] * scale
        rope = (q2[o_rope + hh * D_ROPE:o_rope + (hh + 1) * D_ROPE] * cos2
                + q2[o_rot + hh * D_ROPE:o_rot + (hh + 1) * D_ROPE] * sin2)
        qt_ref[0, hh] = jnp.concatenate([nope, rope, zpad], axis=0).astype(BF16)

    kn = jnp.dot(c_kv, wk_ref[...], preferred_element_type=F32)
    kr = p[:, o4:o4 + HEAD_PAD] * tabk_ref[0]
    kr = kr + pltpu.roll(kr, HEAD_PAD - D_ROPE, axis=1)
    lane = lax.broadcasted_iota(jnp.int32, kr.shape, 1)
    kr = jnp.where((lane >= D_NOPE) & (lane < D_NOPE + D_ROPE), kr, 0.0)
    for hh in range(N_HEADS):
        lo = hh * HEAD_PAD
        k_ref[0, hh] = (kn[:, lo:lo + HEAD_PAD] + kr).astype(BF16)

    vt = lax.dot_general(wvt_ref[...], c_kv, nt, preferred_element_type=F32)
    for hh in range(N_HEADS):
        vt_ref[0, hh, 0:D_V] = vt[hh * D_V:(hh + 1) * D_V].astype(BF16)
        vt_ref[0, hh, D_V:VT_ROWS] = jnp.ones((VT_ROWS - D_V, tm), BF16)


def _proj(x, cos_t, sin_t, tab_k, g_mix, w_in_ext, g_q, w_uqt, g_kv, w_k, w_vt):
    b, s, d = x.shape
    tm = TM_PROJ
    half = D_ROPE // 2
    tok = lambda bi, i: (bi, i, 0)
    tok_t = lambda bi, i: (bi, 0, i)
    return pl.pallas_call(
        _proj_kernel,
        out_shape=(jax.ShapeDtypeStruct((b, N_HEADS, HEAD_PAD, s), BF16),
                   jax.ShapeDtypeStruct((b, N_HEADS, s, HEAD_PAD), BF16),
                   jax.ShapeDtypeStruct((b, N_HEADS, VT_ROWS, s), BF16),
                   jax.ShapeDtypeStruct((b, s, CONV_CH), F32)),
        grid=(b, s // tm),
        in_specs=[pl.BlockSpec((1, tm, d), tok),
                  pl.BlockSpec((1, half, tm), tok_t),
                  pl.BlockSpec((1, half, tm), tok_t),
                  pl.BlockSpec((1, tm, HEAD_PAD), tok),
                  _full(g_mix.shape), _full(w_in_ext.shape), _full(g_q.shape),
                  _full(w_uqt.shape), _full(g_kv.shape), _full(w_k.shape), _full(w_vt.shape)],
        out_specs=(pl.BlockSpec((1, N_HEADS, HEAD_PAD, tm), lambda bi, i: (bi, 0, 0, i)),
                   pl.BlockSpec((1, N_HEADS, tm, HEAD_PAD), lambda bi, i: (bi, 0, i, 0)),
                   pl.BlockSpec((1, N_HEADS, VT_ROWS, tm), lambda bi, i: (bi, 0, 0, i)),
                   pl.BlockSpec((1, tm, CONV_CH), tok)),
        compiler_params=_cparams(2),
        name="proj",
    )(x, cos_t, sin_t, tab_k, g_mix, w_in_ext, g_q, w_uqt, g_kv, w_k, w_vt)


def _attn_kernel(qt_ref, k_ref, vt_ref, yt_ref, s0_ref, s1_ref):
    s_bufs = (s0_ref, s1_ref)
    dyn0 = jnp.minimum(pl.program_id(0), 0)
    tq = TQ_SUB
    n_sub = qt_ref.shape[3] // tq
    s_len = k_ref.shape[2]
    n_chunks = s_len // TK
    groups = TK // 8
    n_units = n_sub * N_HEADS

    def score_chunk(unit, c, m8):
        sb, hh = divmod(unit, N_HEADS)
        qh = qt_ref[0, hh, :, sb * tq:(sb + 1) * tq]
        kc = k_ref[0, hh, c * TK:(c + 1) * TK, :]
        st = jnp.dot(kc, qh, preferred_element_type=F32)
        s_bufs[unit % 2][c * TK:(c + 1) * TK, :] = st
        mc = jnp.max(st.reshape(groups, 8, tq), axis=0)
        return mc if m8 is None else jnp.maximum(m8, mc)

    def value_chunk(unit, c, m, acc):
        hh = unit % N_HEADS
        row0 = pl.multiple_of(dyn0 + c * TK, TK)
        pt = jnp.exp2(s_bufs[unit % 2][pl.ds(row0, TK), :] - m)
        vc = vt_ref[0, hh, :, c * TK:(c + 1) * TK]
        return acc + jnp.dot(vc, pt.astype(BF16), preferred_element_type=F32)

    m8 = None
    for c in range(n_chunks):
        m8 = score_chunk(0, c, m8)
    m = jnp.max(m8, axis=0, keepdims=True)
    for unit in range(n_units):
        sb, hh = divmod(unit, N_HEADS)
        acc = jnp.zeros((VT_ROWS, tq), F32)
        m8 = None
        for c in range(n_chunks):
            if unit + 1 < n_units:
                m8 = score_chunk(unit + 1, c, m8)
            acc = value_chunk(unit, c, m, acc)
        yt_ref[0, hh, :, sb * tq:(sb + 1) * tq] = (acc[:D_V] / acc[D_V:D_V + 1]).astype(BF16)
        if unit + 1 < n_units:
            m = jnp.max(m8, axis=0, keepdims=True)


def _attention(qt, k, vt):
    b, nh, _, s = qt.shape
    return pl.pallas_call(
        _attn_kernel,
        out_shape=jax.ShapeDtypeStruct((b, nh, D_V, s), BF16),
        grid=(b, s // TQ),
        in_specs=[pl.BlockSpec((1, nh, HEAD_PAD, TQ), lambda bi, i: (bi, 0, 0, i)),
                  pl.BlockSpec((1, nh, s, HEAD_PAD), lambda bi, i: (bi, 0, 0, 0)),
                  pl.BlockSpec((1, nh, VT_ROWS, s), lambda bi, i: (bi, 0, 0, 0))],
        out_specs=pl.BlockSpec((1, nh, D_V, TQ), lambda bi, i: (bi, 0, 0, i)),
        scratch_shapes=[pltpu.VMEM((s, TQ_SUB), F32), pltpu.VMEM((s, TQ_SUB), F32)],
        compiler_params=_cparams(2),
        name="attn",
    )(qt, k, vt)


def _mix_kernel(yt_ref, u_ref, up_ref, un_ref, x_ref, wdw_ref, bdw_ref, gln_ref, bln_ref,
                woa_ref, woc_ref, o_ref, sh_ref, conv_ref):
    tm = u_ref.shape[1]
    rows = tm + 2 * CONV_HALO
    i = pl.program_id(1)
    last = pl.num_programs(1) - 1
    sh_ref[0, 0:CONV_HALO] = jnp.where(i > 0, up_ref[0], 0.0)
    sh_ref[0, CONV_HALO:CONV_HALO + tm] = u_ref[0]
    sh_ref[0, CONV_HALO + tm:rows] = jnp.where(i < last, un_ref[0], 0.0)
    ext = sh_ref[0]
    for j in range(1, 8):
        sh_ref[j] = pltpu.roll(ext, rows - j, axis=0)

    base = CONV_HALO - CONV_K // 2
    for cb in range(CONV_CH // 128):
        lanes = slice(cb * 128, (cb + 1) * 128)
        for rb in range(tm // CONV_ROWS):
            r0 = rb * CONV_ROWS
            acc = jnp.broadcast_to(bdw_ref[:, lanes], (CONV_ROWS, 128))
            for kk in range(CONV_K):
                off = base + kk
                j = off % 8
                a0 = off - j + r0
                acc = acc + wdw_ref[kk:kk + 1, lanes] * sh_ref[j, a0:a0 + CONV_ROWS, lanes]
            conv_ref[r0:r0 + CONV_ROWS, lanes] = acc
    acc = conv_ref[...]

    mu = jnp.mean(acc, axis=-1, keepdims=True)
    cen = acc - mu
    var = jnp.mean(cen * cen, axis=-1, keepdims=True)
    y = cen * lax.rsqrt(var + EPS) * gln_ref[...] + bln_ref[...]
    conv = (y * jax.nn.sigmoid(y)).astype(BF16)

    yt = yt_ref[0].reshape(N_HEADS * D_V, tm)
    tn = (((0,), (0,)), ((), ()))
    out = lax.dot_general(yt, woa_ref[...], tn, preferred_element_type=F32)
    out = out + jnp.dot(conv, woc_ref[...], preferred_element_type=F32)
    o_ref[0] = x_ref[0] + out


def _mix(yt, u, x, w_dw, b_dw, g_ln, b_ln, w_oa, w_oc):
    b, s, d = x.shape
    tm = TM_MIX
    r = tm // CONV_HALO
    n_halo = s // CONV_HALO
    tok = lambda bi, i: (bi, i, 0)
    prev = lambda bi, i: (bi, jnp.maximum(i * r - 1, 0), 0)
    nxt = lambda bi, i: (bi, jnp.minimum((i + 1) * r, n_halo - 1), 0)
    return pl.pallas_call(
        _mix_kernel,
        out_shape=jax.ShapeDtypeStruct((b, s, d), F32),
        grid=(b, s // tm),
        in_specs=[pl.BlockSpec((1, N_HEADS, D_V, tm), lambda bi, i: (bi, 0, 0, i)),
                  pl.BlockSpec((1, tm, CONV_CH), tok),
                  pl.BlockSpec((1, CONV_HALO, CONV_CH), prev),
                  pl.BlockSpec((1, CONV_HALO, CONV_CH), nxt),
                  pl.BlockSpec((1, tm, d), tok),
                  _full(w_dw.shape), _full(b_dw.shape), _full(g_ln.shape), _full(b_ln.shape),
                  _full(w_oa.shape), _full(w_oc.shape)],
        out_specs=pl.BlockSpec((1, tm, d), tok),
        scratch_shapes=[pltpu.VMEM((8, tm + 2 * CONV_HALO, CONV_CH), F32),
                        pltpu.VMEM((tm, CONV_CH), F32)],
        compiler_params=_cparams(2),
        name="mix",
    )(yt, u, u, u, x, w_dw, b_dw, g_ln, b_ln, w_oa, w_oc)


def _ffn_kernel(x_ref, xp_ref, xn_ref, g_ref, wup_ref, wdw_ref, bdw_ref, wdn_ref, gfin_ref,
                o_ref, h_ref, *, final_norm):
    tm = x_ref.shape[1]
    i = pl.program_id(1)
    last = pl.num_programs(1) - 1
    g = g_ref[...]
    hp = jnp.where(i > 0, _rms(xp_ref[0], g), 0.0)
    hn = jnp.where(i < last, _rms(xn_ref[0], g), 0.0)
    h_ref[0:FFN_HALO] = hp.astype(BF16)
    h_ref[FFN_HALO:FFN_HALO + tm] = _rms(x_ref[0], g).astype(BF16)
    h_ref[FFN_HALO + tm:FFN_HALO + tm + FFN_HALO] = hn.astype(BF16)
    h = h_ref[...]

    rows = tm + 2 * FFN_HALO

    def conv3(z, col, cw):
        w = wdw_ref[:, col:col + cw]
        lo = FFN_HALO
        prev_rows = pltpu.roll(z, 1, axis=0)[lo:lo + tm]
        next_rows = pltpu.roll(z, rows - 1, axis=0)[lo:lo + tm]
        return (bdw_ref[:, col:col + cw] + w[0:1] * prev_rows + w[1:2] * z[lo:lo + tm]
                + w[2:3] * next_rows)

    acc = jnp.zeros((tm, D_MODEL), F32)
    cg = 0
    while cg < D_FF:
        cw = min(FF_CHUNK, D_FF - cg)
        cv = D_FF + cg
        zg = conv3(jnp.dot(h, wup_ref[:, cg:cg + cw], preferred_element_type=F32), cg, cw)
        zv = conv3(jnp.dot(h, wup_ref[:, cv:cv + cw], preferred_element_type=F32), cv, cw)
        act = (zg * jax.nn.sigmoid(zg) * zv).astype(BF16)
        acc = acc + jnp.dot(act, wdn_ref[cg:cg + cw, :], preferred_element_type=F32)
        cg += cw

    out = x_ref[0] + acc
    if final_norm:
        out = _rms(out, gfin_ref[...])
    o_ref[0] = out


def _ffn(x, g_ffn, w_up, w_dw, b_dw, w_down, g_final, final_norm):
    b, s, d = x.shape
    tm = TM_FFN
    r = tm // FFN_HALO
    n_halo = s // FFN_HALO
    tok = lambda bi, i: (bi, i, 0)
    prev = lambda bi, i: (bi, jnp.maximum(i * r - 1, 0), 0)
    nxt = lambda bi, i: (bi, jnp.minimum((i + 1) * r, n_halo - 1), 0)
    single = dict(pipeline_mode=pl.Buffered(1))
    return pl.pallas_call(
        functools.partial(_ffn_kernel, final_norm=final_norm),
        out_shape=jax.ShapeDtypeStruct((b, s, d), F32),
        grid=(b, s // tm),
        in_specs=[pl.BlockSpec((1, tm, d), tok),
                  pl.BlockSpec((1, FFN_HALO, d), prev),
                  pl.BlockSpec((1, FFN_HALO, d), nxt),
                  _full(g_ffn.shape),
                  pl.BlockSpec(w_up.shape, lambda *_: (0, 0), **single),
                  _full(w_dw.shape), _full(b_dw.shape),
                  pl.BlockSpec(w_down.shape, lambda *_: (0, 0), **single),
                  _full(g_final.shape)],
        out_specs=pl.BlockSpec((1, tm, d), tok),
        scratch_shapes=[pltpu.VMEM((tm + 2 * FFN_HALO, d), BF16)],
        compiler_params=_cparams(2),
        name="ffn_final" if final_norm else "ffn",
    )(x, x, x, g_ffn, w_up, w_dw, b_dw, w_down, g_final)


def _rot_cols(w):
    half = w.shape[-1] // 2
    return jnp.concatenate([-w[..., half:], w[..., :half]], axis=-1)


def _prep_layer(w_in, w_uq, w_ukv, w_o):
    o1 = Q_LORA
    o2 = o1 + KV_LORA
    o3 = o2 + D_ROPE
    w_kr = w_in[:, o2:o3]
    kr_block = jnp.concatenate(
        [jnp.zeros((D_MODEL, D_NOPE), F32), w_kr, _rot_cols(w_kr)], axis=1)
    w_in_ext = jnp.concatenate([w_in[:, :o2], w_in[:, o3:], kr_block], axis=1).astype(BF16)

    wq = w_uq.reshape(Q_LORA, N_HEADS, D_NOPE + D_ROPE)
    q_nope = wq[:, :, :D_NOPE].reshape(Q_LORA, N_HEADS * D_NOPE)
    q_rope = wq[:, :, D_NOPE:].reshape(Q_LORA, N_HEADS * D_ROPE)
    q_rot = _rot_cols(wq[:, :, D_NOPE:]).reshape(Q_LORA, N_HEADS * D_ROPE)
    w_uqt = jnp.concatenate([q_nope, q_rope, q_rot], axis=1).T.astype(BF16)

    wkv = w_ukv.reshape(KV_LORA, N_HEADS, D_NOPE + D_V)
    w_k = jnp.concatenate(
        [wkv[:, :, :D_NOPE], jnp.zeros((KV_LORA, N_HEADS, HEAD_PAD - D_NOPE), F32)],
        axis=2).reshape(KV_LORA, N_HEADS * HEAD_PAD).astype(BF16)
    w_vt = wkv[:, :, D_NOPE:].reshape(KV_LORA, N_HEADS * D_V).T.astype(BF16)

    w_oa = w_o[:N_HEADS * D_V].astype(BF16)
    w_oc = w_o[N_HEADS * D_V:].astype(BF16)
    return w_in_ext, w_uqt, w_k, w_vt, w_oa, w_oc


def kernel(x, positions, g_mix, w_in, g_q, w_uq, g_kv, w_ukv, w_dw_conv, b_dw_conv, g_conv_ln, b_conv_ln, w_o, g_ffn, w_up, w_dw_ffn, b_dw_ffn, w_down, g_final):
    depth = w_in.shape[0]
    cos_t, sin_t, tab_k = _rope_tables(positions)
    row = lambda v: v.reshape(1, -1)
    for l in range(depth):
        w_in_ext, w_uqt, w_k, w_vt, w_oa, w_oc = _prep_layer(w_in[l], w_uq[l], w_ukv[l], w_o[l])
        qt, k, vt, u = _proj(x, cos_t, sin_t, tab_k, row(g_mix[l]), w_in_ext, row(g_q[l]),
                             w_uqt, row(g_kv[l]), w_k, w_vt)
        yt = _attention(qt, k, vt)
        x = _mix(yt, u, x, w_dw_conv[l], row(b_dw_conv[l]), row(g_conv_ln[l]),
                 row(b_conv_ln[l]), w_oa, w_oc)
        x = _ffn(x, row(g_ffn[l]), w_up[l].astype(BF16), w_dw_ffn[l], row(b_dw_ffn[l]),
                 w_down[l].astype(BF16), row(g_final), final_norm=(l == depth - 1))
    return x
```

```python
import functools
import math

import jax
import jax.numpy as jnp
from jax import lax
from jax.experimental import pallas as pl
from jax.experimental.pallas import tpu as pltpu

D_MODEL = 1024
N_HEADS = 8
D_NOPE = 64
D_ROPE = 32
D_V = 64
Q_LORA = 384
KV_LORA = 256
CONV_CH = 512
CONV_K = 31
D_FF = 2816
FFN_K = 3
ROPE_THETA = 10000.0
EPS = 1e-6

HEAD_PAD = 128
VT_ROWS = 80
LOG2E = 1.4426950408889634
IN_EXT = Q_LORA + KV_LORA + 2 * CONV_CH + HEAD_PAD
CONV_HALO = 16
CONV_ROWS = 256
FFN_HALO = 8
FF_CHUNK = 1536

TM_PROJ = 1024
TQ = 512
TQ_SUB = 256
TK = 256
TM_MIX = 512
TM_FFN = 1024

VMEM_LIMIT = 56 * 1024 * 1024

F32 = jnp.float32
BF16 = jnp.bfloat16


def _cparams(n_axes):
    return pltpu.CompilerParams(
        dimension_semantics=("arbitrary",) * n_axes, vmem_limit_bytes=VMEM_LIMIT)


def _full(shape):
    return pl.BlockSpec(shape, lambda *_: (0,) * len(shape))


def _rms(v, g):
    return v * lax.rsqrt(jnp.mean(v * v, axis=-1, keepdims=True) + EPS) * g


def _table_kernel(pos_ref, invf_ref, cos_ref, sin_ref, tabk_ref):
    n_b = pos_ref.shape[0]
    s = pos_ref.shape[2]
    zeros = jnp.zeros((D_NOPE, s), F32)
    for b in range(n_b):
        ang = pos_ref[b].astype(F32) * invf_ref[...]
        c = jnp.cos(ang)
        sn = jnp.sin(ang)
        cos_ref[b] = c
        sin_ref[b] = sn
        tabk_ref[b] = jnp.concatenate([zeros, c, c, sn, sn], axis=0).T


def _rope_tables(positions):
    b, s = positions.shape
    half = D_ROPE // 2
    inv_freq = 1.0 / (ROPE_THETA ** (jnp.arange(0, D_ROPE, 2, dtype=F32) / D_ROPE))
    invf = jnp.broadcast_to(inv_freq[:, None], (half, s))
    return pl.pallas_call(
        _table_kernel,
        out_shape=(jax.ShapeDtypeStruct((b, half, s), F32),
                   jax.ShapeDtypeStruct((b, half, s), F32),
                   jax.ShapeDtypeStruct((b, s, HEAD_PAD), F32)),
        grid=(1,),
        in_specs=[_full((b, 1, s)), _full((half, s))],
        out_specs=(_full((b, half, s)), _full((b, half, s)), _full((b, s, HEAD_PAD))),
        compiler_params=_cparams(1),
        name="rope_tables",
    )(positions.reshape(b, 1, s), invf)


def _proj_kernel(x_ref, cos_ref, sin_ref, tabk_ref, gmix_ref, win_ref, gq_ref, wuqt_ref,
                 gkv_ref, wk_ref, wvt_ref, qt_ref, k_ref, vt_ref, u_ref):
    tm = x_ref.shape[1]
    scale = LOG2E / math.sqrt(D_NOPE + D_ROPE)
    h = _rms(x_ref[0], gmix_ref[...]).astype(BF16)
    p = jnp.dot(h, win_ref[...], preferred_element_type=F32)
    o1 = Q_LORA
    o2 = o1 + KV_LORA
    o3 = o2 + CONV_CH
    o4 = o3 + CONV_CH
    c_q = _rms(p[:, :o1], gq_ref[...]).astype(BF16)
    c_kv = _rms(p[:, o1:o2], gkv_ref[...]).astype(BF16)

    a = p[:, o2:o3]
    gt = p[:, o3:o4]
    u_ref[0] = a * jax.nn.sigmoid(gt)

    nt = (((1,), (1,)), ((), ()))
    q2 = lax.dot_general(wuqt_ref[...], c_q, nt, preferred_element_type=F32)
    c = cos_ref[0] * scale
    s = sin_ref[0] * scale
    cos2 = jnp.concatenate([c, c], axis=0)
    sin2 = jnp.concatenate([s, s], axis=0)
    o_rope = N_HEADS * D_NOPE
    o_rot = o_rope + N_HEADS * D_ROPE
    zpad = jnp.zeros((HEAD_PAD - D_NOPE - D_ROPE, tm), F32)
    for hh in range(N_HEADS):
        nope = q2[hh * D_NOPE:(hh + 1) * D_NOPE] * scale
        rope = (q2[o_rope + hh * D_ROPE:o_rope + (hh + 1) * D_ROPE] * cos2
                + q2[o_rot + hh * D_ROPE:o_rot + (hh + 1) * D_ROPE] * sin2)
        qt_ref[0, hh] = jnp.concatenate([nope, rope, zpad], axis=0).astype(BF16)

    kn = jnp.dot(c_kv, wk_ref[...], preferred_element_type=F32)
    kr = p[:, o4:o4 + HEAD_PAD] * tabk_ref[0]
    kr = kr + pltpu.roll(kr, HEAD_PAD - D_ROPE, axis=1)
    lane = lax.broadcasted_iota(jnp.int32, kr.shape, 1)
    kr = jnp.where((lane >= D_NOPE) & (lane < D_NOPE + D_ROPE), kr, 0.0)
    for hh in range(N_HEADS):
        lo = hh * HEAD_PAD
        k_ref[0, hh] = (kn[:, lo:lo + HEAD_PAD] + kr).astype(BF16)

    vt = lax.dot_general(wvt_ref[...], c_kv, nt, preferred_element_type=F32)
    for hh in range(N_HEADS):
        vt_ref[0, hh, 0:D_V] = vt[hh * D_V:(hh + 1) * D_V].astype(BF16)
        vt_ref[0, hh, D_V:VT_ROWS] = jnp.ones((VT_ROWS - D_V, tm), BF16)


def _proj(x, cos_t, sin_t, tab_k, g_mix, w_in_ext, g_q, w_uqt, g_kv, w_k, w_vt):
    b, s, d = x.shape
    tm = TM_PROJ
    half = D_ROPE // 2
    tok = lambda bi, i: (bi, i, 0)
    tok_t = lambda bi, i: (bi, 0, i)
    return pl.pallas_call(
        _proj_kernel,
        out_shape=(jax.ShapeDtypeStruct((b, N_HEADS, HEAD_PAD, s), BF16),
                   jax.ShapeDtypeStruct((b, N_HEADS, s, HEAD_PAD), BF16),
                   jax.ShapeDtypeStruct((b, N_HEADS, VT_ROWS, s), BF16),
                   jax.ShapeDtypeStruct((b, s, CONV_CH), F32)),
        grid=(b, s // tm),
        in_specs=[pl.BlockSpec((1, tm, d), tok),
                  pl.BlockSpec((1, half, tm), tok_t),
                  pl.BlockSpec((1, half, tm), tok_t),
                  pl.BlockSpec((1, tm, HEAD_PAD), tok),
                  _full(g_mix.shape), _full(w_in_ext.shape), _full(g_q.shape),
                  _full(w_uqt.shape), _full(g_kv.shape), _full(w_k.shape), _full(w_vt.shape)],
        out_specs=(pl.BlockSpec((1, N_HEADS, HEAD_PAD, tm), lambda bi, i: (bi, 0, 0, i)),
                   pl.BlockSpec((1, N_HEADS, tm, HEAD_PAD), lambda bi, i: (bi, 0, i, 0)),
                   pl.BlockSpec((1, N_HEADS, VT_ROWS, tm), lambda bi, i: (bi, 0, 0, i)),
                   pl.BlockSpec((1, tm, CONV_CH), tok)),
        compiler_params=_cparams(2),
        name="proj",
    )(x, cos_t, sin_t, tab_k, g_mix, w_in_ext, g_q, w_uqt, g_kv, w_k, w_vt)


def _attn_kernel(qt_ref, k_ref, vt_ref, yt_ref, s0_ref, s1_ref):
    s_bufs = (s0_ref, s1_ref)
    dyn0 = jnp.minimum(pl.program_id(0), 0)
    tq = TQ_SUB
    n_sub = qt_ref.shape[3] // tq
    s_len = k_ref.shape[2]
    n_chunks = s_len // TK
    groups = TK // 8
    n_units = n_sub * N_HEADS

    def score_chunk(unit, c, m8):
        sb, hh = divmod(unit, N_HEADS)
        qh = qt_ref[0, hh, :, sb * tq:(sb + 1) * tq]
        kc = k_ref[0, hh, c * TK:(c + 1) * TK, :]
        st = jnp.dot(kc, qh, preferred_element_type=F32)
        s_bufs[unit % 2][c * TK:(c + 1) * TK, :] = st
        mc = jnp.max(st.reshape(groups, 8, tq), axis=0)
        return mc if m8 is None else jnp.maximum(m8, mc)

    def value_chunk(unit, c, m, acc):
        hh = unit % N_HEADS
        row0 = pl.multiple_of(dyn0 + c * TK, TK)
        pt = jnp.exp2(s_bufs[unit % 2][pl.ds(row0, TK), :] - m)
        vc = vt_ref[0, hh, :, c * TK:(c + 1) * TK]
        return acc + jnp.dot(vc, pt.astype(BF16), preferred_element_type=F32)

    m8 = None
    for c in range(n_chunks):
        m8 = score_chunk(0, c, m8)
    m = jnp.max(m8, axis=0, keepdims=True)
    for unit in range(n_units):
        sb, hh = divmod(unit, N_HEADS)
        acc = jnp.zeros((VT_ROWS, tq), F32)
        m8 = None
        for c in range(n_chunks):
            if unit + 1 < n_units:
                m8 = score_chunk(unit + 1, c, m8)
            acc = value_chunk(unit, c, m, acc)
        yt_ref[0, hh, :, sb * tq:(sb + 1) * tq] = (acc[:D_V] / acc[D_V:D_V + 1]).astype(BF16)
        if unit + 1 < n_units:
            m = jnp.max(m8, axis=0, keepdims=True)


def _attention(qt, k, vt):
    b, nh, _, s = qt.shape
    return pl.pallas_call(
        _attn_kernel,
        out_shape=jax.ShapeDtypeStruct((b, nh, D_V, s), BF16),
        grid=(b, s // TQ),
        in_specs=[pl.BlockSpec((1, nh, HEAD_PAD, TQ), lambda bi, i: (bi, 0, 0, i)),
                  pl.BlockSpec((1, nh, s, HEAD_PAD), lambda bi, i: (bi, 0, 0, 0)),
                  pl.BlockSpec((1, nh, VT_ROWS, s), lambda bi, i: (bi, 0, 0, 0))],
        out_specs=pl.BlockSpec((1, nh, D_V, TQ), lambda bi, i: (bi, 0, 0, i)),
        scratch_shapes=[pltpu.VMEM((s, TQ_SUB), F32), pltpu.VMEM((s, TQ_SUB), F32)],
        compiler_params=_cparams(2),
        name="attn",
    )(qt, k, vt)


def _mix_kernel(yt_ref, u_ref, up_ref, un_ref, x_ref, wdw_ref, bdw_ref, gln_ref, bln_ref,
                woa_ref, woc_ref, o_ref, sh_ref, conv_ref):
    tm = u_ref.shape[1]
    rows = tm + 2 * CONV_HALO
    i = pl.program_id(1)
    last = pl.num_programs(1) - 1
    sh_ref[0, 0:CONV_HALO] = jnp.where(i > 0, up_ref[0], 0.0)
    sh_ref[0, CONV_HALO:CONV_HALO + tm] = u_ref[0]
    sh_ref[0, CONV_HALO + tm:rows] = jnp.where(i < last, un_ref[0], 0.0)
    ext = sh_ref[0]
    for j in range(1, 8):
        sh_ref[j] = pltpu.roll(ext, rows - j, axis=0)

    base = CONV_HALO - CONV_K // 2
    for cb in range(CONV_CH // 128):
        lanes = slice(cb * 128, (cb + 1) * 128)
        for rb in range(tm // CONV_ROWS):
            r0 = rb * CONV_ROWS
            acc = jnp.broadcast_to(bdw_ref[:, lanes], (CONV_ROWS, 128))
            for kk in range(CONV_K):
                off = base + kk
                j = off % 8
                a0 = off - j + r0
                acc = acc + wdw_ref[kk:kk + 1, lanes] * sh_ref[j, a0:a0 + CONV_ROWS, lanes]
            conv_ref[r0:r0 + CONV_ROWS, lanes] = acc
    acc = conv_ref[...]

    mu = jnp.mean(acc, axis=-1, keepdims=True)
    cen = acc - mu
    var = jnp.mean(cen * cen, axis=-1, keepdims=True)
    y = cen * lax.rsqrt(var + EPS) * gln_ref[...] + bln_ref[...]
    conv = (y * jax.nn.sigmoid(y)).astype(BF16)

    yt = yt_ref[0].reshape(N_HEADS * D_V, tm)
    tn = (((0,), (0,)), ((), ()))
    out = lax.dot_general(yt, woa_ref[...], tn, preferred_element_type=F32)
    out = out + jnp.dot(conv, woc_ref[...], preferred_element_type=F32)
    o_ref[0] = x_ref[0] + out


def _mix(yt, u, x, w_dw, b_dw, g_ln, b_ln, w_oa, w_oc):
    b, s, d = x.shape
    tm = TM_MIX
    r = tm // CONV_HALO
    n_halo = s // CONV_HALO
    tok = lambda bi, i: (bi, i, 0)
    prev = lambda bi, i: (bi, jnp.maximum(i * r - 1, 0), 0)
    nxt = lambda bi, i: (bi, jnp.minimum((i + 1) * r, n_halo - 1), 0)
    return pl.pallas_call(
        _mix_kernel,
        out_shape=jax.ShapeDtypeStruct((b, s, d), F32),
        grid=(b, s // tm),
        in_specs=[pl.BlockSpec((1, N_HEADS, D_V, tm), lambda bi, i: (bi, 0, 0, i)),
                  pl.BlockSpec((1, tm, CONV_CH), tok),
                  pl.BlockSpec((1, CONV_HALO, CONV_CH), prev),
                  pl.BlockSpec((1, CONV_HALO, CONV_CH), nxt),
                  pl.BlockSpec((1, tm, d), tok),
                  _full(w_dw.shape), _full(b_dw.shape), _full(g_ln.shape), _full(b_ln.shape),
                  _full(w_oa.shape), _full(w_oc.shape)],
        out_specs=pl.BlockSpec((1, tm, d), tok),
        scratch_shapes=[pltpu.VMEM((8, tm + 2 * CONV_HALO, CONV_CH), F32),
                        pltpu.VMEM((tm, CONV_CH), F32)],
        compiler_params=_cparams(2),
        name="mix",
    )(yt, u, u, u, x, w_dw, b_dw, g_ln, b_ln, w_oa, w_oc)


def _ffn_kernel(x_ref, xp_ref, xn_ref, g_ref, wup_ref, wdw_ref, bdw_ref, wdn_ref, gfin_ref,
                o_ref, h_ref, *, final_norm):
    tm = x_ref.shape[1]
    i = pl.program_id(1)
    last = pl.num_programs(1) - 1
    g = g_ref[...]
    hp = jnp.where(i > 0, _rms(xp_ref[0], g), 0.0)
    hn = jnp.where(i < last, _rms(xn_ref[0], g), 0.0)
    h_ref[0:FFN_HALO] = hp.astype(BF16)
    h_ref[FFN_HALO:FFN_HALO + tm] = _rms(x_ref[0], g).astype(BF16)
    h_ref[FFN_HALO + tm:FFN_HALO + tm + FFN_HALO] = hn.astype(BF16)
    h = h_ref[...]

    rows = tm + 2 * FFN_HALO

    def conv3(z, col, cw):
        w = wdw_ref[:, col:col + cw]
        lo = FFN_HALO
        prev_rows = pltpu.roll(z, 1, axis=0)[lo:lo + tm]
        next_rows = pltpu.roll(z, rows - 1, axis=0)[lo:lo + tm]
        return (bdw_ref[:, col:col + cw] + w[0:1] * prev_rows + w[1:2] * z[lo:lo + tm]
                + w[2:3] * next_rows)

    acc = jnp.zeros((tm, D_MODEL), F32)
    cg = 0
    while cg < D_FF:
        cw = min(FF_CHUNK, D_FF - cg)
        cv = D_FF + cg
        zg = conv3(jnp.dot(h, wup_ref[:, cg:cg + cw], preferred_element_type=F32), cg, cw)
        zv = conv3(jnp.dot(h, wup_ref[:, cv:cv + cw], preferred_element_type=F32), cv, cw)
        act = (zg * jax.nn.sigmoid(zg) * zv).astype(BF16)
        acc = acc + jnp.dot(act, wdn_ref[cg:cg + cw, :], preferred_element_type=F32)
        cg += cw

    out = x_ref[0] + acc
    if final_norm:
        out = _rms(out, gfin_ref[...])
    o_ref[0] = out


def _ffn(x, g_ffn, w_up, w_dw, b_dw, w_down, g_final, final_norm):
    b, s, d = x.shape
    tm = TM_FFN
    r = tm // FFN_HALO
    n_halo = s // FFN_HALO
    tok = lambda bi, i: (bi, i, 0)
    prev = lambda bi, i: (bi, jnp.maximum(i * r - 1, 0), 0)
    nxt = lambda bi, i: (bi, jnp.minimum((i + 1) * r, n_halo - 1), 0)
    single = dict(pipeline_mode=pl.Buffered(1))
    return pl.pallas_call(
        functools.partial(_ffn_kernel, final_norm=final_norm),
        out_shape=jax.ShapeDtypeStruct((b, s, d), F32),
        grid=(b, s // tm),
        in_specs=[pl.BlockSpec((1, tm, d), tok),
                  pl.BlockSpec((1, FFN_HALO, d), prev),
                  pl.BlockSpec((1, FFN_HALO, d), nxt),
                  _full(g_ffn.shape),
                  pl.BlockSpec(w_up.shape, lambda *_: (0, 0), **single),
                  _full(w_dw.shape), _full(b_dw.shape),
                  pl.BlockSpec(w_down.shape, lambda *_: (0, 0), **single),
                  _full(g_final.shape)],
        out_specs=pl.BlockSpec((1, tm, d), tok),
        scratch_shapes=[pltpu.VMEM((tm + 2 * FFN_HALO, d), BF16)],
        compiler_params=_cparams(2),
        name="ffn_final" if final_norm else "ffn",
    )(x, x, x, g_ffn, w_up, w_dw, b_dw, w_down, g_final)


def _rot_cols(w):
    half = w.shape[-1] // 2
    return jnp.concatenate([-w[..., half:], w[..., :half]], axis=-1)


def _prep_layer(w_in, w_uq, w_ukv, w_o):
    o1 = Q_LORA
    o2 = o1 + KV_LORA
    o3 = o2 + D_ROPE
    w_kr = w_in[:, o2:o3]
    kr_block = jnp.concatenate(
        [jnp.zeros((D_MODEL, D_NOPE), F32), w_kr, _rot_cols(w_kr)], axis=1)
    w_in_ext = jnp.concatenate([w_in[:, :o2], w_in[:, o3:], kr_block], axis=1).astype(BF16)

    wq = w_uq.reshape(Q_LORA, N_HEADS, D_NOPE + D_ROPE)
    q_nope = wq[:, :, :D_NOPE].reshape(Q_LORA, N_HEADS * D_NOPE)
    q_rope = wq[:, :, D_NOPE:].reshape(Q_LORA, N_HEADS * D_ROPE)
    q_rot = _rot_cols(wq[:, :, D_NOPE:]).reshape(Q_LORA, N_HEADS * D_ROPE)
    w_uqt = jnp.concatenate([q_nope, q_rope, q_rot], axis=1).T.astype(BF16)

    wkv = w_ukv.reshape(KV_LORA, N_HEADS, D_NOPE + D_V)
    w_k = jnp.concatenate(
        [wkv[:, :, :D_NOPE], jnp.zeros((KV_LORA, N_HEADS, HEAD_PAD - D_NOPE), F32)],
        axis=2).reshape(KV_LORA, N_HEADS * HEAD_PAD).astype(BF16)
    w_vt = wkv[:, :, D_NOPE:].reshape(KV_LORA, N_HEADS * D_V).T.astype(BF16)

    w_oa = w_o[:N_HEADS * D_V].astype(BF16)
    w_oc = w_o[N_HEADS * D_V:].astype(BF16)
    return w_in_ext, w_uqt, w_k, w_vt, w_oa, w_oc


def kernel(x, positions, g_mix, w_in, g_q, w_uq, g_kv, w_ukv, w_dw_conv, b_dw_conv, g_conv_ln, b_conv_ln, w_o, g_ffn, w_up, w_dw_ffn, b_dw_ffn, w_down, g_final):
    depth = w_in.shape[0]
    cos_t, sin_t, tab_k = _rope_tables(positions)
    row = lambda v: v.reshape(1, -1)
    for l in range(depth):
        w_in_ext, w_uqt, w_k, w_vt, w_oa, w_oc = _prep_layer(w_in[l], w_uq[l], w_ukv[l], w_o[l])
        qt, k, vt, u = _proj(x, cos_t, sin_t, tab_k, row(g_mix[l]), w_in_ext, row(g_q[l]),
                             w_uqt, row(g_kv[l]), w_k, w_vt)
        yt = _attention(qt, k, vt)
        x = _mix(yt, u, x, w_dw_conv[l], row(b_dw_conv[l]), row(g_conv_ln[l]),
                 row(b_conv_ln[l]), w_oa, w_oc)
        x = _ffn(x, row(g_ffn[l]), w_up[l].astype(BF16), w_dw_ffn[l], row(b_dw_ffn[l]),
                 w_down[l].astype(BF16), row(g_final), final_norm=(l == depth - 1))
    return x
```
